```python
import jax
import jax.numpy as jnp
from jax import lax
import numpy as np


D_MODEL = 1024
BATCH = 4
SEQ = 4096
DEPTH = 2

N_MIXERS = 2
N_ATTN_LAYERS = (DEPTH + 1) // 2
N_RWKV_LAYERS = DEPTH // 2
HEAD_DIM = 64
N_HEADS = D_MODEL // HEAD_DIM
N_META = 16
BLOCK = 128
D_FF = ((-(-8 * D_MODEL // 3)) + 255) // 256 * 256
DECAY_LORA = max(32, int(round(1.8 * D_MODEL ** 0.5 / 32)) * 32)
AAA_LORA = max(32, int(round(1.8 * D_MODEL ** 0.5 / 32)) * 32)
MV_LORA = max(32, int(round(1.3 * D_MODEL ** 0.5 / 32)) * 32)
GATE_LORA = max(32, int(round(0.6 * D_MODEL ** 0.8 / 32)) * 32)
RMS_EPS = 1e-6
GN_EPS = 64e-5
L2_EPS = 1e-12

kernel_name = 'hybrid_stickbreak_rwkv7_block'


def rms_norm(x, g):
    xf = x.astype(jnp.float32)
    y = xf * lax.rsqrt(jnp.mean(xf * xf, axis=-1, keepdims=True) + RMS_EPS)
    return (y * g.astype(jnp.float32)).astype(x.dtype)


def swiglu(h, w_in, w_out):
    gate, up = jnp.split(jnp.einsum('btd,df->btf', h, w_in), 2, axis=-1)
    return jnp.einsum('btf,fd->btd', jax.nn.silu(gate) * up, w_out)


def stick_breaking_attention(h, w_qkv, w_o):
    B, T, D = h.shape
    q, k, v = jnp.split(jnp.einsum('btd,de->bte', h, w_qkv), 3, axis=-1)
    pad = (-N_META) % BLOCK
    tail = (-(T + pad)) % BLOCK
    Tp = T + pad + tail

    def heads(u):
        u = u.reshape(B, T, N_HEADS, HEAD_DIM).transpose(0, 2, 1, 3)
        return jnp.pad(u, ((0, 0), (0, 0), (pad, tail), (0, 0)))

    qh, kh, vh = heads(q), heads(k), heads(v)
    scale = HEAD_DIM ** -0.5
    outs = []
    for blk in range(Tp // BLOCK):
        L = (blk + 1) * BLOCK
        qb = qh[:, :, blk * BLOCK:L]
        kb = kh[:, :, :L]
        vb = vh[:, :, :L]
        z = jnp.einsum('bhtd,bhsd->bhts', qb, kb).astype(jnp.float32) * scale
        t_idx = blk * BLOCK + jnp.arange(BLOCK)
        s_idx = jnp.arange(L)
        mask = (s_idx[None, :] < t_idx[:, None]) & (s_idx[None, :] >= pad)
        log_beta = jax.nn.log_sigmoid(z)
        log_rest = jnp.where(mask, jax.nn.log_sigmoid(-z), 0.0)
        later = lax.cumsum(log_rest, axis=3, reverse=True) - log_rest
        att = jnp.where(mask, jnp.exp(log_beta + later), 0.0)
        outs.append(jnp.einsum('bhts,bhsd->bhtd', att.astype(vb.dtype), vb))
    o = jnp.concatenate(outs, axis=2)[:, :, pad:pad + T]
    o = o.transpose(0, 2, 1, 3).reshape(B, T, D)
    return jnp.einsum('btd,de->bte', o, w_o), v


def rwkv7_time_mix(h, v_first, mu, w_rkv, w_o, w0, w1, w2, a0, a1, a2, v0, v1, v2,
                   g1, g2, k_k, k_a, r_k, ln_x_w, ln_x_b):
    B, T, D = h.shape
    xx = jnp.pad(h, ((0, 0), (1, 0), (0, 0)))[:, :-1] - h
    xs = h[None] + xx[None] * mu[:, None, None, :]
    rkv = jnp.einsum('nbtd,nde->nbte', xs[:3], w_rkv)
    r, k, v = rkv[0], rkv[1], rkv[2]
    xv, xw, xa, xg = xs[2], xs[3], xs[4], xs[5]
    w_log = -jax.nn.softplus(-(w0 + jnp.einsum('btl,ld->btd', jnp.tanh(xw @ w1), w2)).astype(jnp.float32)) - 0.5
    decay = jnp.exp(-jnp.exp(w_log))
    v = v + (v_first - v) * jax.nn.sigmoid(v0 + (xv @ v1) @ v2)
    a = jax.nn.sigmoid(a0 + (xa @ a1) @ a2)
    g = jax.nn.sigmoid(xg @ g1) @ g2

    def hs(u):
        return u.astype(jnp.float32).reshape(B, T, N_HEADS, HEAD_DIM)

    kk = hs(k * k_k)
    kk = kk / jnp.maximum(jnp.sqrt(jnp.sum(kk * kk, axis=-1, keepdims=True)), L2_EPS)
    k = k * (1.0 + (a - 1.0) * k_a)
    r_h, k_h, v_h, a_h, w_h = hs(r), hs(k), hs(v), hs(a), hs(decay)

    def step(S, inp):
        r_t, w_t, k_t, v_t, a_t, b_t = inp
        sa = jnp.einsum('bhij,bhj->bhi', S, a_t)
        S = (S * w_t[:, :, None, :] + sa[..., None] * b_t[:, :, None, :]
             + v_t[..., None] * k_t[:, :, None, :])
        return S, jnp.einsum('bhij,bhj->bhi', S, r_t)

    seq_in = tuple(u.transpose(1, 0, 2, 3) for u in (r_h, w_h, k_h, v_h, -kk, kk * a_h))
    S0 = jnp.zeros((B, N_HEADS, HEAD_DIM, HEAD_DIM), jnp.float32)
    _, y = lax.scan(step, S0, seq_in)
    y = y.transpose(1, 0, 2, 3)
    mean = jnp.mean(y, axis=-1, keepdims=True)
    var = jnp.mean(jnp.square(y - mean), axis=-1, keepdims=True)
    y = ((y - mean) * lax.rsqrt(var + GN_EPS) * ln_x_w.astype(jnp.float32).reshape(N_HEADS, HEAD_DIM)
         + ln_x_b.astype(jnp.float32).reshape(N_HEADS, HEAD_DIM))
    bonus = jnp.sum(r_h * k_h * r_k.astype(jnp.float32), axis=-1, keepdims=True) * v_h
    y = (y + bonus).reshape(B, T, D).astype(h.dtype)
    return jnp.einsum('btd,de->bte', y * g, w_o)


def setup_inputs(seed: int = 0) -> dict:
    key = jax.random.key(seed)
    ks = jax.random.split(key, 32)
    D, H, N, F = D_MODEL, N_HEADS, HEAD_DIM, D_FF
    NA, NR = N_ATTN_LAYERS, N_RWKV_LAYERS

    def nrm(k, shape, scale):
        return jax.random.normal(k, shape, jnp.float32) * scale

    return {
        'x': nrm(ks[0], (BATCH, SEQ, D), 1.0),
        'meta_tokens': nrm(ks[1], (N_META, D), 1.0),
        'attn_norm': 1.0 + nrm(ks[2], (NA, D), 0.05),
        'w_qkv': nrm(ks[3], (NA, D, 3 * D), D ** -0.5),
        'w_o_attn': nrm(ks[4], (NA, D, D), D ** -0.5),
        'rwkv_norm': 1.0 + nrm(ks[5], (NR, D), 0.05),
        'rwkv_mu': jax.random.uniform(ks[6], (NR, 6, D), jnp.float32),
        'w_rkv': nrm(ks[7], (NR, 3, D, D), D ** -0.5),
        'w_o_rwkv': nrm(ks[8], (NR, D, D), D ** -0.5),
        'w0': jax.random.uniform(ks[9], (NR, D), jnp.float32, minval=-6.5, maxval=-1.5),
        'w1': nrm(ks[10], (NR, D, DECAY_LORA), D ** -0.5),
        'w2': nrm(ks[11], (NR, DECAY_LORA, D), DECAY_LORA ** -0.5),
        'a0': nrm(ks[12], (NR, D), 0.1),
        'a1': nrm(ks[13], (NR, D, AAA_LORA), D ** -0.5),
        'a2': nrm(ks[14], (NR, AAA_LORA, D), AAA_LORA ** -0.5),
        'v0': 1.0 + nrm(ks[15], (NR, D), 0.1),
        'v1': nrm(ks[16], (NR, D, MV_LORA), D ** -0.5),
        'v2': nrm(ks[17], (NR, MV_LORA, D), MV_LORA ** -0.5),
        'g1': nrm(ks[18], (NR, D, GATE_LORA), D ** -0.5),
        'g2': nrm(ks[19], (NR, GATE_LORA, D), GATE_LORA ** -0.5),
        'k_k': 0.85 + nrm(ks[20], (NR, D), 0.05),
        'k_a': 1.0 + nrm(ks[21], (NR, D), 0.05),
        'r_k': nrm(ks[22], (NR, H, N), 0.1),
        'ln_x_w': 1.0 + nrm(ks[23], (NR, D), 0.05),
        'ln_x_b': nrm(ks[24], (NR, D), 0.01),
        'ffn_norm': 1.0 + nrm(ks[25], (DEPTH, D), 0.05),
        'w_ffn_in': nrm(ks[26], (DEPTH, D, 2 * F), D ** -0.5),
        'w_ffn_out': nrm(ks[27], (DEPTH, F, D), F ** -0.5),
        'final_norm': 1.0 + nrm(ks[28], (D,), 0.05),
    }


def reference(x, meta_tokens, attn_norm, w_qkv, w_o_attn, rwkv_norm, rwkv_mu, w_rkv, w_o_rwkv,
              w0, w1, w2, a0, a1, a2, v0, v1, v2, g1, g2, k_k, k_a, r_k, ln_x_w, ln_x_b,
              ffn_norm, w_ffn_in, w_ffn_out, final_norm):
    B = x.shape[0]
    meta = jnp.broadcast_to(meta_tokens[None].astype(x.dtype), (B, N_META, D_MODEL))
    h = jnp.concatenate([meta, x], axis=1)
    v_first = None
    for i in range(DEPTH):
        j = i // N_MIXERS
        if i % N_MIXERS == 0:
            mix, v_layer = stick_breaking_attention(rms_norm(h, attn_norm[j]), w_qkv[j], w_o_attn[j])
            if v_first is None:
                v_first = v_layer
        else:
            mix = rwkv7_time_mix(rms_norm(h, rwkv_norm[j]), v_first, rwkv_mu[j], w_rkv[j], w_o_rwkv[j],
                                 w0[j], w1[j], w2[j], a0[j], a1[j], a2[j], v0[j], v1[j], v2[j],
                                 g1[j], g2[j], k_k[j], k_a[j], r_k[j], ln_x_w[j], ln_x_b[j])
        h = h + mix
        h = h + swiglu(rms_norm(h, ffn_norm[i]), w_ffn_in[i], w_ffn_out[i])
    return rms_norm(h[:, N_META:], final_norm)
```

```python
import functools

import jax
import jax.numpy as jnp
from jax import lax
from jax.experimental import pallas as pl
from jax.experimental.pallas import tpu as pltpu

F32 = jnp.float32
BF16 = jnp.bfloat16

HEAD_DIM = 64
N_META = 16
ATT_BLOCK = 128
LANES = 128
CHUNK = 64
RMS_EPS = 1e-6
GN_EPS = 64e-5
L2_EPS = 1e-12
VMEM_LIMIT = 56 * 1024 * 1024

_NN = (((1,), (0,)), ((), ()))
_NT = (((1,), (1,)), ((), ()))
_TN = (((0,), (0,)), ((), ()))


def _dg(a, b, dn=_NN):
    return lax.dot_general(a, b, dn, preferred_element_type=F32)


def _split(x, n):
    parts = []
    for _ in range(n):
        p = x.astype(BF16)
        parts.append(p)
        x = x - p.astype(F32)
    return parts


def _mm(a, b, dn=_NN, passes=3):
    if passes == 1:
        return _dg(a.astype(BF16), b.astype(BF16), dn)
    ah, al = _split(a, 2)
    bh, bl = _split(b, 2)
    return _dg(ah, bh, dn) + (_dg(ah, bl, dn) + _dg(al, bh, dn))


def _mm_exact_lhs(a_bf, b, dn=_NN, n=3):
    parts = _split(b, n)
    out = _dg(a_bf, parts[-1], dn)
    for p in parts[-2::-1]:
        out = out + _dg(a_bf, p, dn)
    return out


def _mm_exact_rhs(a, b_bf, dn=_NN, n=3):
    parts = _split(a, n)
    out = _dg(parts[-1], b_bf, dn)
    for p in parts[-2::-1]:
        out = out + _dg(p, b_bf, dn)
    return out


def _rms(x, g):
    return x * lax.rsqrt(jnp.mean(x * x, axis=-1, keepdims=True) + RMS_EPS) * g


def _softplus_neg_abs(z):
    return jnp.log1p(jnp.exp(-jnp.abs(z)))


def _sigmoid(x):
    return 1.0 / (1.0 + jnp.exp(-x))


def _const_spec(shape):
    nd = len(shape)
    return pl.BlockSpec(shape, lambda *_: (0,) * nd, pipeline_mode=pl.Buffered(1))


def _qkv_kernel(x_ref, g_ref, w_ref, qkv_ref, v_ref, *, d):
    xb = _rms(x_ref[...], g_ref[...]).astype(BF16)
    for j in range(3):
        y = _dg(xb, w_ref[:, j * d:(j + 1) * d])
        if j == 0:
            y = y * (HEAD_DIM ** -0.5)
        qkv_ref[:, j * d:(j + 1) * d] = y.astype(BF16)
        if j == 2:
            v_ref[...] = y


def _qkv_call(h2d, g, w_bf, tm):
    m, d = h2d.shape
    return pl.pallas_call(
        functools.partial(_qkv_kernel, d=d),
        grid=(m // tm,),
        in_specs=[pl.BlockSpec((tm, d), lambda i: (i, 0)),
                  _const_spec((1, d)),
                  _const_spec((d, 3 * d))],
        out_specs=[pl.BlockSpec((tm, 3 * d), lambda i: (i, 0)),
                   pl.BlockSpec((tm, d), lambda i: (i, 0))],
        out_shape=[jax.ShapeDtypeStruct((m, 3 * d), BF16),
                   jax.ShapeDtypeStruct((m, d), F32)],
        compiler_params=pltpu.CompilerParams(
            dimension_semantics=("parallel",), vmem_limit_bytes=VMEM_LIMIT),
        name="qkv",
    )(h2d, g, w_bf)


def _attn_kernel(q_ref, k_ref, v_ref, o_ref, *, pad):
    blk = ATT_BLOCK
    i = pl.program_id(2)
    q = q_ref[...]
    row = lax.broadcasted_iota(jnp.int32, (blk, blk), 0)
    col = lax.broadcasted_iota(jnp.int32, (blk, blk), 1)
    lane_lo = lax.broadcasted_iota(jnp.int32, (blk, LANES), 1) < HEAD_DIM
    zero_q = jnp.zeros_like(q)
    q_heads = (jnp.where(lane_lo, q, zero_q), jnp.where(lane_lo, zero_q, q))
    upper = (row > col).astype(BF16)

    def body(jj, carry):
        acc, c0, c1 = carry
        j = i - jj
        start = pl.multiple_of(j * blk, blk)
        kt = k_ref[pl.ds(start, blk), :]
        vt = v_ref[pl.ds(start, blk), :]
        s_idx = j * blk + col
        mask = (s_idx < i * blk + row) & (s_idx >= pad)

        def head(qh, c):
            z = _dg(qh, kt, _NT)
            sp = _softplus_neg_abs(z)
            log_beta = jnp.minimum(z, 0.0) - sp
            log_rest = jnp.where(mask, -jnp.maximum(z, 0.0) - sp, 0.0)
            later = _mm_exact_rhs(log_rest, upper, n=2) + c
            att = jnp.where(mask, jnp.exp(log_beta + later), 0.0)
            return _dg(att.astype(BF16), vt), c + jnp.sum(log_rest, axis=1, keepdims=True)

        o0, c0 = head(q_heads[0], c0)
        o1, c1 = head(q_heads[1], c1)
        return acc + jnp.where(lane_lo, o0, o1), c0, c1

    zc = jnp.zeros((blk, 1), F32)
    acc, _, _ = lax.fori_loop(0, i + 1, body, (jnp.zeros((blk, LANES), F32), zc, zc))
    o_ref[...] = acc.astype(BF16)


def _attn_call(qkv3d, pad):
    b, tp, d3 = qkv3d.shape
    d = d3 // 3
    ng = d // LANES
    blk = ATT_BLOCK
    return pl.pallas_call(
        functools.partial(_attn_kernel, pad=pad),
        grid=(b, ng, tp // blk),
        in_specs=[pl.BlockSpec((None, blk, LANES), lambda bi, p, i: (bi, i, p)),
                  pl.BlockSpec((None, tp, LANES), lambda bi, p, i: (bi, 0, ng + p)),
                  pl.BlockSpec((None, tp, LANES), lambda bi, p, i: (bi, 0, 2 * ng + p))],
        out_specs=pl.BlockSpec((None, blk, LANES), lambda bi, p, i: (bi, i, p)),
        out_shape=jax.ShapeDtypeStruct((b, tp, d), BF16),
        compiler_params=pltpu.CompilerParams(
            dimension_semantics=("parallel", "parallel", "arbitrary"),
            vmem_limit_bytes=VMEM_LIMIT),
        name="attention",
    )(qkv3d, qkv3d, qkv3d)


def _mix_ffn_kernel(*refs, f, final):
    if final:
        h_ref, o_ref, wo_ref, g_ref, win_ref, wout_ref, fg_ref, out_ref = refs
    else:
        h_ref, o_ref, wo_ref, g_ref, win_ref, wout_ref, out_ref = refs
    h1 = h_ref[...] + _dg(o_ref[...], wo_ref[...])
    xb = _rms(h1, g_ref[...]).astype(BF16)
    gu = _dg(xb, win_ref[...])
    gate = gu[:, :f]
    act = (gate * _sigmoid(gate) * gu[:, f:]).astype(BF16)
    out = h1 + _dg(act, wout_ref[...])
    if final:
        out = _rms(out, fg_ref[...])
    out_ref[...] = out


def _mix_ffn_call(h2d, o2d, wo_bf, g, win_bf, wout_bf, final_g, tm):
    m, d = h2d.shape
    f = wout_bf.shape[0]
    final = final_g is not None
    in_specs = [pl.BlockSpec((tm, d), lambda i: (i, 0)),
                pl.BlockSpec((tm, d), lambda i: (i, 0)),
                _const_spec((d, d)),
                _const_spec((1, d)),
                _const_spec((d, 2 * f)),
                _const_spec((f, d))]
    args = [h2d, o2d, wo_bf, g, win_bf, wout_bf]
    if final:
        in_specs.append(_const_spec((1, d)))
        args.append(final_g)
    return pl.pallas_call(
        functools.partial(_mix_ffn_kernel, f=f, final=final),
        grid=(m // tm,),
        in_specs=in_specs,
        out_specs=pl.BlockSpec((tm, d), lambda i: (i, 0)),
        out_shape=jax.ShapeDtypeStruct((m, d), F32),
        compiler_params=pltpu.CompilerParams(
            dimension_semantics=("parallel",), vmem_limit_bytes=VMEM_LIMIT),
        name="mix_ffn_final" if final else "mix_ffn",
    )(*args)


def _group_sum(x, bd):
    return _mm_exact_rhs(x, bd, n=3)


def _head_block_diag(n):
    r = lax.broadcasted_iota(jnp.int32, (n, n), 0) // HEAD_DIM
    c = lax.broadcasted_iota(jnp.int32, (n, n), 1) // HEAD_DIM
    return (r == c).astype(BF16)


def _rwkv_prep_kernel(h_ref, hprev_ref, vf_ref, g_ref, mu_ref, vec_ref, wrkv_ref,
                      w1_ref, w2_ref, a1_ref, a2_ref, v1_ref, v2_ref, g1_ref, g2_ref,
                      r_ref, k_ref, v_ref, lw_ref, kk_ref, b_ref, gate_ref, *, tm, tp):
    i = pl.program_id(0)
    g = g_ref[...]
    hn = _rms(h_ref[...], g)
    prev_last = _rms(hprev_ref[...], g)[7:8, :]
    local = lax.broadcasted_iota(jnp.int32, (tm, 1), 0)
    prev = jnp.where(local == 0, prev_last, pltpu.roll(hn, 1, axis=0))
    xx = jnp.where((i * tm + local) % tp == 0, 0.0, prev - hn)

    def mix(n):
        return (hn + xx * mu_ref[n:n + 1, :]).astype(BF16)

    w0, a0, v0 = vec_ref[0:1, :], vec_ref[1:2, :], vec_ref[2:3, :]
    k_k, k_a = vec_ref[3:4, :], vec_ref[4:5, :]

    r_ref[...] = _dg(mix(0), wrkv_ref[0])
    k = _dg(mix(1), wrkv_ref[1])
    xv = mix(2)
    v = _dg(xv, wrkv_ref[2])

    wl = w0 + _dg(jnp.tanh(_dg(mix(3), w1_ref[...])).astype(BF16), w2_ref[...])
    w_log = -(jnp.maximum(-wl, 0.0) + _softplus_neg_abs(wl)) - 0.5
    lw_ref[...] = -jnp.exp(w_log)

    v_gate = _sigmoid(v0 + _dg(_dg(xv, v1_ref[...]).astype(BF16), v2_ref[...]))
    v_ref[...] = v + (vf_ref[...] - v) * v_gate
    a = _sigmoid(a0 + _dg(_dg(mix(4), a1_ref[...]).astype(BF16), a2_ref[...]))
    gate_ref[...] = _dg(_sigmoid(_dg(mix(5), g1_ref[...])).astype(BF16), g2_ref[...])

    bd = _head_block_diag(LANES)
    kk = k * k_k
    for p in range(kk.shape[1] // LANES):
        sl = slice(p * LANES, (p + 1) * LANES)
        kkp = kk[:, sl]
        norm = jnp.maximum(jnp.sqrt(_group_sum(kkp * kkp, bd)), L2_EPS)
        kkp = kkp / norm
        kk_ref[:, sl] = kkp
        b_ref[:, sl] = kkp * a[:, sl]
    k_ref[...] = k * (1.0 + (a - 1.0) * k_a)


def _rwkv_prep_call(h2d, vf2d, g, mu, vecs, wrkv_bf, loras, tm, tp):
    m, d = h2d.shape
    row_spec = pl.BlockSpec((tm, d), lambda i: (i, 0))
    in_specs = [row_spec,
                pl.BlockSpec((8, d), lambda i: (jnp.maximum(i * (tm // 8) - 1, 0), 0)),
                row_spec,
                _const_spec((1, d)), _const_spec(mu.shape), _const_spec(vecs.shape),
                _const_spec(wrkv_bf.shape)]
    in_specs += [_const_spec(w.shape) for w in loras]
    return pl.pallas_call(
        functools.partial(_rwkv_prep_kernel, tm=tm, tp=tp),
        grid=(m // tm,),
        in_specs=in_specs,
        out_specs=[row_spec] * 7,
        out_shape=[jax.ShapeDtypeStruct((m, d), F32)] * 7,
        compiler_params=pltpu.CompilerParams(
            dimension_semantics=("parallel",), vmem_limit_bytes=VMEM_LIMIT),
        name="rwkv_prep",
    )(h2d, h2d, vf2d, g, mu, vecs, wrkv_bf, *loras)


def _stack_heads(x, lane_lo):
    zero = jnp.zeros_like(x)
    return jnp.concatenate([jnp.where(lane_lo, x, zero), jnp.where(lane_lo, zero, x)], axis=0)


def _rwkv_pair(r, k, v, lw, kk, b, cum, state, passes):
    c = CHUNK
    lane_lo = lax.broadcasted_iota(jnp.int32, (c, LANES), 1) < HEAD_DIM
    total = cum[c - 1:c, :]
    e_incl = jnp.exp(cum)
    e_inv = jnp.exp(-cum)
    e_rem = jnp.exp(total - cum)
    a_t = -kk * jnp.exp(cum - lw)
    r_t = r * e_incl
    st = functools.partial(_stack_heads, lane_lo=lane_lo)
    s_v = st(v)
    gram = _mm(jnp.concatenate([st(a_t), st(r_t)], axis=0),
               jnp.concatenate([st(b * e_inv), st(k * e_inv)], axis=0), _NT, passes)
    n = 2 * c
    row = lax.broadcasted_iota(jnp.int32, (n, n), 0) % c
    col = lax.broadcasted_iota(jnp.int32, (n, n), 1) % c
    strict = col < row
    incl = col <= row
    a_ab = jnp.where(strict, gram[:n, :n], 0.0)
    a_ak = jnp.where(strict, gram[:n, n:], 0.0)
    m_rb = jnp.where(incl, gram[n:, :n], 0.0)
    m_rk = jnp.where(incl, gram[n:, n:], 0.0)

    eye = (lax.broadcasted_iota(jnp.int32, (n, n), 0)
           == lax.broadcasted_iota(jnp.int32, (n, n), 1)).astype(F32)
    inv = eye + a_ab
    power = a_ab
    steps = (c - 1).bit_length() - 1
    for _ in range(steps):
        power = _mm(power, power, _NN, passes)
        inv = inv + _mm(inv, power, _NN, passes)

    x1 = _mm(jnp.concatenate([a_t, r_t], axis=0), state, _NN, passes)
    s_u = _mm(inv, st(x1[:c]) + _mm(a_ak, s_v, _NN, passes), _NN, passes)
    s_uv = jnp.concatenate([s_u, s_v], axis=0)
    s_y = st(x1[c:]) + _mm(jnp.concatenate([m_rb, m_rk], axis=1), s_uv, _NN, passes)
    y = s_y[:c] + s_y[c:]

    decay_rows = jnp.exp(_mm_exact_rhs(lw, jnp.ones((c, LANES), BF16), _TN, n=3))
    new_state = state * decay_rows + _mm(
        jnp.concatenate([st(b * e_rem), st(k * e_rem)], axis=0), s_uv, _TN, passes)
    return y, new_state


def _rwkv_scan_kernel(r_ref, k_ref, v_ref, lw_ref, kk_ref, b_ref, gate_ref,
                      rk_ref, lnw_ref, lnb_ref, out_ref, state_ref, *, passes):
    c = CHUNK

    @pl.when(pl.program_id(2) == 0)
    def _():
        state_ref[...] = jnp.zeros_like(state_ref)

    lw = lw_ref[...]
    rowc = lax.broadcasted_iota(jnp.int32, (c, c), 0)
    colc = lax.broadcasted_iota(jnp.int32, (c, c), 1)
    cum = _mm_exact_lhs((colc <= rowc).astype(BF16), lw, n=3)
    bd = _head_block_diag(LANES)
    inv_n = 1.0 / HEAD_DIM
    for p in range(lw.shape[1] // LANES):
        sl = slice(p * LANES, (p + 1) * LANES)
        r, k, v = r_ref[:, sl], k_ref[:, sl], v_ref[:, sl]
        y, new_state = _rwkv_pair(r, k, v, lw[:, sl], kk_ref[:, sl], b_ref[:, sl],
                                  cum[:, sl], state_ref[p], passes)
        state_ref[p] = new_state
        mean = _group_sum(y, bd) * inv_n
        dev = y - mean
        var = _group_sum(dev * dev, bd) * inv_n
        yn = dev * lax.rsqrt(var + GN_EPS) * lnw_ref[:, sl] + lnb_ref[:, sl]
        bonus = _group_sum(r * k * rk_ref[:, sl], bd) * v
        out_ref[:, sl] = ((yn + bonus) * gate_ref[:, sl]).astype(BF16)


def _rwkv_scan_call(seqs, rk, lnw, lnb, lane_group, passes):
    b, tp, d = seqs[0].shape
    c = CHUNK
    seq_spec = pl.BlockSpec((None, c, lane_group), lambda bi, l, ci: (bi, ci, l))
    vec_spec = pl.BlockSpec((1, lane_group), lambda bi, l, ci: (0, l))
    return pl.pallas_call(
        functools.partial(_rwkv_scan_kernel, passes=passes),
        grid=(b, d // lane_group, tp // c),
        in_specs=[seq_spec] * 7 + [vec_spec] * 3,
        out_specs=seq_spec,
        out_shape=jax.ShapeDtypeStruct((b, tp, d), BF16),
        scratch_shapes=[pltpu.VMEM((lane_group // LANES, LANES, LANES), F32)],
        compiler_params=pltpu.CompilerParams(
            dimension_semantics=("parallel", "parallel", "arbitrary"),
            vmem_limit_bytes=VMEM_LIMIT),
        name="rwkv_scan",
    )(*seqs, rk, lnw, lnb)


def kernel(x, meta_tokens, attn_norm, w_qkv, w_o_attn, rwkv_norm, rwkv_mu, w_rkv, w_o_rwkv,
           w0, w1, w2, a0, a1, a2, v0, v1, v2, g1, g2, k_k, k_a, r_k, ln_x_w, ln_x_b,
           ffn_norm, w_ffn_in, w_ffn_out, final_norm):
    b, seq, d = x.shape
    n_meta = meta_tokens.shape[0]
    pad = (-n_meta) % ATT_BLOCK
    tp = pad + n_meta + seq
    assert tp % ATT_BLOCK == 0 and tp % CHUNK == 0 and d % LANES == 0
    m = b * tp
    tm = 384 if m % 384 == 0 else 128
    tm_prep = 256 if m % 256 == 0 else 128

    def row(vec):
        return vec.reshape(1, d).astype(F32)

    meta = jnp.broadcast_to(meta_tokens[None].astype(x.dtype), (b, n_meta, d))
    h = jnp.concatenate([jnp.zeros((b, pad, d), x.dtype), meta, x], axis=1).reshape(m, d)

    qkv, v_first = _qkv_call(h, row(attn_norm[0]), w_qkv[0].astype(BF16), tm)
    o = _attn_call(qkv.reshape(b, tp, 3 * d), pad).reshape(m, d)
    h = _mix_ffn_call(h, o, w_o_attn[0].astype(BF16), row(ffn_norm[0]),
                      w_ffn_in[0].astype(BF16), w_ffn_out[0].astype(BF16), None, tm)

    vecs = jnp.concatenate([row(w0[0]), row(a0[0]), row(v0[0]), row(k_k[0]), row(k_a[0]),
                            jnp.zeros((3, d), F32)], axis=0)
    loras = [w.astype(BF16) for w in (w1[0], w2[0], a1[0], a2[0], v1[0], v2[0], g1[0], g2[0])]
    seqs = _rwkv_prep_call(h, v_first, row(rwkv_norm[0]), rwkv_mu[0], vecs,
                           w_rkv[0].astype(BF16), loras, tm_prep, tp)
    y = _rwkv_scan_call([s.reshape(b, tp, d) for s in seqs], row(r_k[0]), row(ln_x_w[0]),
                        row(ln_x_b[0]), 2 * LANES, 3).reshape(m, d)
    h = _mix_ffn_call(h, y, w_o_rwkv[0].astype(BF16), row(ffn_norm[1]),
                      w_ffn_in[1].astype(BF16), w_ffn_out[1].astype(BF16),
                      row(final_norm), tm)
    return h.reshape(b, tp, d)[:, pad + n_meta:]
```

```python
import functools

import jax
import jax.numpy as jnp
from jax import lax
from jax.experimental import pallas as pl
from jax.experimental.pallas import tpu as pltpu

F32 = jnp.float32
BF16 = jnp.bfloat16

HEAD_DIM = 64
N_META = 16
ATT_BLOCK = 128
LANES = 128
CHUNK = 64
RMS_EPS = 1e-6
GN_EPS = 64e-5
L2_EPS = 1e-12
EXP_UNDERFLOW = -88.0
VMEM_LIMIT = 56 * 1024 * 1024

_NN = (((1,), (0,)), ((), ()))
_NT = (((1,), (1,)), ((), ()))
_TN = (((0,), (0,)), ((), ()))


def _dg(a, b, dn=_NN):
    return lax.dot_general(a, b, dn, preferred_element_type=F32)


def _split(x, n):
    parts = []
    for _ in range(n):
        p = x.astype(BF16)
        parts.append(p)
        x = x - p.astype(F32)
    return parts


def _mm(a, b, dn=_NN, passes=3):
    if passes == 1:
        return _dg(a.astype(BF16), b.astype(BF16), dn)
    ah, al = _split(a, 2)
    bh, bl = _split(b, 2)
    return _dg(ah, bh, dn) + (_dg(ah, bl, dn) + _dg(al, bh, dn))


def _mm_exact_lhs(a_bf, b, dn=_NN, n=3):
    parts = _split(b, n)
    out = _dg(a_bf, parts[-1], dn)
    for p in parts[-2::-1]:
        out = out + _dg(a_bf, p, dn)
    return out


def _mm_exact_rhs(a, b_bf, dn=_NN, n=3):
    parts = _split(a, n)
    out = _dg(parts[-1], b_bf, dn)
    for p in parts[-2::-1]:
        out = out + _dg(p, b_bf, dn)
    return out


def _rms(x, g):
    return x * lax.rsqrt(jnp.mean(x * x, axis=-1, keepdims=True) + RMS_EPS) * g


def _softplus_neg_abs(z):
    return jnp.log1p(jnp.exp(-jnp.abs(z)))


def _sigmoid(x):
    return 1.0 / (1.0 + jnp.exp(-x))


def _const_spec(shape):
    nd = len(shape)
    return pl.BlockSpec(shape, lambda *_: (0,) * nd, pipeline_mode=pl.Buffered(1))


def _qkv_kernel(x_ref, g_ref, w_ref, qkv_ref, v_ref, *, d):
    xb = _rms(x_ref[...], g_ref[...]).astype(BF16)
    for j in range(3):
        y = _dg(xb, w_ref[:, j * d:(j + 1) * d])
        if j == 0:
            y = y * (HEAD_DIM ** -0.5)
        qkv_ref[:, j * d:(j + 1) * d] = y.astype(BF16)
        if j == 2:
            v_ref[...] = y


def _qkv_call(h2d, g, w_bf, tm):
    m, d = h2d.shape
    return pl.pallas_call(
        functools.partial(_qkv_kernel, d=d),
        grid=(m // tm,),
        in_specs=[pl.BlockSpec((tm, d), lambda i: (i, 0)),
                  _const_spec((1, d)),
                  _const_spec((d, 3 * d))],
        out_specs=[pl.BlockSpec((tm, 3 * d), lambda i: (i, 0)),
                   pl.BlockSpec((tm, d), lambda i: (i, 0))],
        out_shape=[jax.ShapeDtypeStruct((m, 3 * d), BF16),
                   jax.ShapeDtypeStruct((m, d), F32)],
        compiler_params=pltpu.CompilerParams(
            dimension_semantics=("parallel",), vmem_limit_bytes=VMEM_LIMIT),
        name="qkv",
    )(h2d, g, w_bf)


def _attn_kernel(q_ref, k_ref, v_ref, o_ref, *, pad):
    blk = ATT_BLOCK
    n_groups = q_ref.shape[1] // LANES
    i = pl.program_id(2)
    row = lax.broadcasted_iota(jnp.int32, (blk, blk), 0)
    col = lax.broadcasted_iota(jnp.int32, (blk, blk), 1)
    lane_lo = lax.broadcasted_iota(jnp.int32, (blk, LANES), 1) < HEAD_DIM
    q_heads = []
    for p in range(n_groups):
        q = q_ref[:, p * LANES:(p + 1) * LANES]
        zero_q = jnp.zeros_like(q)
        q_heads += [jnp.where(lane_lo, q, zero_q), jnp.where(lane_lo, zero_q, q)]
    upper = (row > col).astype(BF16)

    def cond(carry):
        jj, c_max = carry[0], carry[1]
        return (jj <= i) & (c_max > EXP_UNDERFLOW)

    def body(carry):
        jj, _, accs, cs = carry
        j = i - jj
        start = pl.multiple_of(j * blk, blk)
        s_idx = j * blk + col
        mask = (s_idx < i * blk + row) & (s_idx >= pad)
        heads = range(2 * n_groups)
        kts = [k_ref[pl.ds(start, blk), p * LANES:(p + 1) * LANES] for p in range(n_groups)]
        vts = [v_ref[pl.ds(start, blk), p * LANES:(p + 1) * LANES] for p in range(n_groups)]
        zs = [_dg(q_heads[h], kts[h // 2], _NT) for h in heads]
        sps = [_softplus_neg_abs(zs[h]) for h in heads]
        log_rest = [jnp.where(mask, -jnp.maximum(zs[h], 0.0) - sps[h], 0.0) for h in heads]
        later = [_mm_exact_rhs(log_rest[h], upper, n=2) for h in heads]
        att = [jnp.where(mask, jnp.exp(jnp.minimum(zs[h], 0.0) - sps[h] + later[h] + cs[h]), 0.0)
               for h in heads]
        outs = [_dg(att[h].astype(BF16), vts[h // 2]) for h in heads]
        new_cs = [cs[h] + jnp.sum(log_rest[h], axis=1, keepdims=True) for h in heads]
        new_accs = [accs[p] + jnp.where(lane_lo, outs[2 * p], outs[2 * p + 1])
                    for p in range(n_groups)]
        c_all = new_cs[0]
        for c in new_cs[1:]:
            c_all = jnp.maximum(c_all, c)
        return jj + 1, jnp.max(c_all), tuple(new_accs), tuple(new_cs)

    init = (jnp.int32(0), jnp.float32(0.0),
            tuple(jnp.zeros((blk, LANES), F32) for _ in range(n_groups)),
            tuple(jnp.zeros((blk, 1), F32) for _ in range(2 * n_groups)))
    accs = lax.while_loop(cond, body, init)[2]
    for p in range(n_groups):
        o_ref[:, p * LANES:(p + 1) * LANES] = accs[p].astype(BF16)


def _attn_call(qkv3d, pad, lane_set):
    b, tp, d3 = qkv3d.shape
    d = d3 // 3
    ng = d // lane_set
    blk = ATT_BLOCK
    return pl.pallas_call(
        functools.partial(_attn_kernel, pad=pad),
        grid=(b, ng, tp // blk),
        in_specs=[pl.BlockSpec((None, blk, lane_set), lambda bi, p, i: (bi, i, p)),
                  pl.BlockSpec((None, tp, lane_set), lambda bi, p, i: (bi, 0, ng + p)),
                  pl.BlockSpec((None, tp, lane_set), lambda bi, p, i: (bi, 0, 2 * ng + p))],
        out_specs=pl.BlockSpec((None, blk, lane_set), lambda bi, p, i: (bi, i, p)),
        out_shape=jax.ShapeDtypeStruct((b, tp, d), BF16),
        compiler_params=pltpu.CompilerParams(
            dimension_semantics=("parallel", "parallel", "arbitrary"),
            vmem_limit_bytes=VMEM_LIMIT),
        name="attention",
    )(qkv3d, qkv3d, qkv3d)


def _mix_ffn_kernel(*refs, f, final):
    if final:
        h_ref, o_ref, wo_ref, g_ref, win_ref, wout_ref, fg_ref, out_ref = refs
    else:
        h_ref, o_ref, wo_ref, g_ref, win_ref, wout_ref, out_ref = refs
    h1 = h_ref[...] + _dg(o_ref[...], wo_ref[...])
    xb = _rms(h1, g_ref[...]).astype(BF16)
    gu = _dg(xb, win_ref[...])
    gate = gu[:, :f]
    act = (gate * _sigmoid(gate) * gu[:, f:]).astype(BF16)
    out = h1 + _dg(act, wout_ref[...])
    if final:
        out = _rms(out, fg_ref[...])
    out_ref[...] = out


def _mix_ffn_call(h2d, o2d, wo_bf, g, win_bf, wout_bf, final_g, tm):
    m, d = h2d.shape
    f = wout_bf.shape[0]
    final = final_g is not None
    in_specs = [pl.BlockSpec((tm, d), lambda i: (i, 0)),
                pl.BlockSpec((tm, d), lambda i: (i, 0)),
                _const_spec((d, d)),
                _const_spec((1, d)),
                _const_spec((d, 2 * f)),
                _const_spec((f, d))]
    args = [h2d, o2d, wo_bf, g, win_bf, wout_bf]
    if final:
        in_specs.append(_const_spec((1, d)))
        args.append(final_g)
    return pl.pallas_call(
        functools.partial(_mix_ffn_kernel, f=f, final=final),
        grid=(m // tm,),
        in_specs=in_specs,
        out_specs=pl.BlockSpec((tm, d), lambda i: (i, 0)),
        out_shape=jax.ShapeDtypeStruct((m, d), F32),
        compiler_params=pltpu.CompilerParams(
            dimension_semantics=("parallel",), vmem_limit_bytes=VMEM_LIMIT),
        name="mix_ffn_final" if final else "mix_ffn",
    )(*args)


def _group_sum(x, bd):
    return _mm_exact_rhs(x, bd, n=3)


def _head_block_diag(n):
    r = lax.broadcasted_iota(jnp.int32, (n, n), 0) // HEAD_DIM
    c = lax.broadcasted_iota(jnp.int32, (n, n), 1) // HEAD_DIM
    return (r == c).astype(BF16)


def _rwkv_prep_kernel(h_ref, hprev_ref, vf_ref, g_ref, mu_ref, vec_ref, wrkv_ref,
                      w1_ref, w2_ref, a1_ref, a2_ref, v1_ref, v2_ref, g1_ref, g2_ref,
                      r_ref, k_ref, v_ref, lw_ref, kk_ref, b_ref, gate_ref, *, tm, tp):
    i = pl.program_id(0)
    g = g_ref[...]
    hn = _rms(h_ref[...], g)
    prev_last = _rms(hprev_ref[...], g)[7:8, :]
    local = lax.broadcasted_iota(jnp.int32, (tm, 1), 0)
    prev = jnp.where(local == 0, prev_last, pltpu.roll(hn, 1, axis=0))
    xx = jnp.where((i * tm + local) % tp == 0, 0.0, prev - hn)

    def mix(n):
        return (hn + xx * mu_ref[n:n + 1, :]).astype(BF16)

    w0, a0, v0 = vec_ref[0:1, :], vec_ref[1:2, :], vec_ref[2:3, :]
    k_k, k_a = vec_ref[3:4, :], vec_ref[4:5, :]

    r_ref[...] = _dg(mix(0), wrkv_ref[0])
    k = _dg(mix(1), wrkv_ref[1])
    xv = mix(2)
    v = _dg(xv, wrkv_ref[2])

    wl = w0 + _dg(jnp.tanh(_dg(mix(3), w1_ref[...])).astype(BF16), w2_ref[...])
    w_log = -(jnp.maximum(-wl, 0.0) + _softplus_neg_abs(wl)) - 0.5
    lw_ref[...] = -jnp.exp(w_log)

    v_gate = _sigmoid(v0 + _dg(_dg(xv, v1_ref[...]).astype(BF16), v2_ref[...]))
    v_ref[...] = v + (vf_ref[...] - v) * v_gate
    a = _sigmoid(a0 + _dg(_dg(mix(4), a1_ref[...]).astype(BF16), a2_ref[...]))
    gate_ref[...] = _dg(_sigmoid(_dg(mix(5), g1_ref[...])).astype(BF16), g2_ref[...])

    bd = _head_block_diag(LANES)
    kk = k * k_k
    for p in range(kk.shape[1] // LANES):
        sl = slice(p * LANES, (p + 1) * LANES)
        kkp = kk[:, sl]
        norm = jnp.maximum(jnp.sqrt(_group_sum(kkp * kkp, bd)), L2_EPS)
        kkp = kkp / norm
        kk_ref[:, sl] = kkp
        b_ref[:, sl] = kkp * a[:, sl]
    k_ref[...] = k * (1.0 + (a - 1.0) * k_a)


def _rwkv_prep_call(h2d, vf2d, g, mu, vecs, wrkv_bf, loras, tm, tp):
    m, d = h2d.shape
    row_spec = pl.BlockSpec((tm, d), lambda i: (i, 0))
    in_specs = [row_spec,
                pl.BlockSpec((8, d), lambda i: (jnp.maximum(i * (tm // 8) - 1, 0), 0)),
                row_spec,
                _const_spec((1, d)), _const_spec(mu.shape), _const_spec(vecs.shape),
                _const_spec(wrkv_bf.shape)]
    in_specs += [_const_spec(w.shape) for w in loras]
    return pl.pallas_call(
        functools.partial(_rwkv_prep_kernel, tm=tm, tp=tp),
        grid=(m // tm,),
        in_specs=in_specs,
        out_specs=[row_spec] * 7,
        out_shape=[jax.ShapeDtypeStruct((m, d), F32)] * 7,
        compiler_params=pltpu.CompilerParams(
            dimension_semantics=("parallel",), vmem_limit_bytes=VMEM_LIMIT),
        name="rwkv_prep",
    )(h2d, h2d, vf2d, g, mu, vecs, wrkv_bf, *loras)


def _stack_heads(x, lane_lo):
    zero = jnp.zeros_like(x)
    return jnp.concatenate([jnp.where(lane_lo, x, zero), jnp.where(lane_lo, zero, x)], axis=0)


def _rwkv_chunk(rs, ks, vs, lws, kks, bs, cums, states, passes):
    c = CHUNK
    n = 2 * c
    groups = range(len(rs))
    lane_lo = lax.broadcasted_iota(jnp.int32, (c, LANES), 1) < HEAD_DIM
    st = functools.partial(_stack_heads, lane_lo=lane_lo)
    row = lax.broadcasted_iota(jnp.int32, (n, n), 0)
    col = lax.broadcasted_iota(jnp.int32, (n, n), 1)
    strict = (col % c) < (row % c)
    incl = (col % c) <= (row % c)
    eye = (row == col).astype(F32)
    ones = jnp.ones((c, LANES), BF16)

    def mm(a, b, dn=_NN):
        return _mm(a, b, dn, passes)

    e_inv = [jnp.exp(-cums[p]) for p in groups]
    a_t = [-kks[p] * jnp.exp(cums[p] - lws[p]) for p in groups]
    r_t = [rs[p] * jnp.exp(cums[p]) for p in groups]
    s_v = [st(vs[p]) for p in groups]
    gram = [mm(jnp.concatenate([st(a_t[p]), st(r_t[p])], axis=0),
               jnp.concatenate([st(bs[p] * e_inv[p]), st(ks[p] * e_inv[p])], axis=0), _NT)
            for p in groups]
    x1 = [mm(jnp.concatenate([a_t[p], r_t[p]], axis=0), states[p]) for p in groups]
    decay_rows = [jnp.exp(_mm_exact_rhs(lws[p], ones, _TN, n=3)) for p in groups]
    a_ab = [jnp.where(strict, gram[p][:n, :n], 0.0) for p in groups]
    rhs = [st(x1[p][:c]) + mm(jnp.where(strict, gram[p][:n, n:], 0.0), s_v[p]) for p in groups]

    inv = [eye + a_ab[p] for p in groups]
    power = a_ab
    for _ in range((c - 1).bit_length() - 1):
        power = [mm(power[p], power[p]) for p in groups]
        inv = [inv[p] + mm(inv[p], power[p]) for p in groups]

    s_uv = [jnp.concatenate([mm(inv[p], rhs[p]), s_v[p]], axis=0) for p in groups]
    incl2 = jnp.concatenate([incl, incl], axis=1)
    s_y = [st(x1[p][c:]) + mm(jnp.where(incl2, gram[p][n:, :], 0.0), s_uv[p]) for p in groups]
    ys = [s_y[p][:c] + s_y[p][c:] for p in groups]
    e_rem = [jnp.exp(cums[p][c - 1:c, :] - cums[p]) for p in groups]
    new_states = [states[p] * decay_rows[p]
                  + mm(jnp.concatenate([st(bs[p] * e_rem[p]), st(ks[p] * e_rem[p])], axis=0),
                       s_uv[p], _TN)
                  for p in groups]
    return ys, new_states


def _rwkv_scan_kernel(r_ref, k_ref, v_ref, lw_ref, kk_ref, b_ref, gate_ref,
                      rk_ref, lnw_ref, lnb_ref, out_ref, state_ref, *, passes):
    c = CHUNK

    @pl.when(pl.program_id(2) == 0)
    def _():
        state_ref[...] = jnp.zeros_like(state_ref)

    lw = lw_ref[...]
    rowc = lax.broadcasted_iota(jnp.int32, (c, c), 0)
    colc = lax.broadcasted_iota(jnp.int32, (c, c), 1)
    cum = _mm_exact_lhs((colc <= rowc).astype(BF16), lw, n=3)
    groups = range(lw.shape[1] // LANES)
    sls = [slice(p * LANES, (p + 1) * LANES) for p in groups]
    rs = [r_ref[:, sl] for sl in sls]
    ks = [k_ref[:, sl] for sl in sls]
    vs = [v_ref[:, sl] for sl in sls]
    ys, new_states = _rwkv_chunk(rs, ks, vs, [lw[:, sl] for sl in sls],
                                 [kk_ref[:, sl] for sl in sls], [b_ref[:, sl] for sl in sls],
                                 [cum[:, sl] for sl in sls], [state_ref[p] for p in groups], passes)
    for p in groups:
        state_ref[p] = new_states[p]
    bd = _head_block_diag(LANES)
    inv_n = 1.0 / HEAD_DIM
    means = [_group_sum(ys[p], bd) * inv_n for p in groups]
    devs = [ys[p] - means[p] for p in groups]
    variances = [_group_sum(devs[p] * devs[p], bd) * inv_n for p in groups]
    bonus = [_group_sum(rs[p] * ks[p] * rk_ref[:, sls[p]], bd) * vs[p] for p in groups]
    for p in groups:
        sl = sls[p]
        yn = devs[p] * lax.rsqrt(variances[p] + GN_EPS) * lnw_ref[:, sl] + lnb_ref[:, sl]
        out_ref[:, sl] = ((yn + bonus[p]) * gate_ref[:, sl]).astype(BF16)


def _rwkv_scan_call(seqs, rk, lnw, lnb, lane_group, passes):
    b, tp, d = seqs[0].shape
    c = CHUNK
    seq_spec = pl.BlockSpec((None, c, lane_group), lambda bi, l, ci: (bi, ci, l))
    vec_spec = pl.BlockSpec((1, lane_group), lambda bi, l, ci: (0, l))
    return pl.pallas_call(
        functools.partial(_rwkv_scan_kernel, passes=passes),
        grid=(b, d // lane_group, tp // c),
        in_specs=[seq_spec] * 7 + [vec_spec] * 3,
        out_specs=seq_spec,
        out_shape=jax.ShapeDtypeStruct((b, tp, d), BF16),
        scratch_shapes=[pltpu.VMEM((lane_group // LANES, LANES, LANES), F32)],
        compiler_params=pltpu.CompilerParams(
            dimension_semantics=("parallel", "parallel", "arbitrary"),
            vmem_limit_bytes=VMEM_LIMIT),
        name="rwkv_scan",
    )(*seqs, rk, lnw, lnb)


def kernel(x, meta_tokens, attn_norm, w_qkv, w_o_attn, rwkv_norm, rwkv_mu, w_rkv, w_o_rwkv,
           w0, w1, w2, a0, a1, a2, v0, v1, v2, g1, g2, k_k, k_a, r_k, ln_x_w, ln_x_b,
           ffn_norm, w_ffn_in, w_ffn_out, final_norm):
    b, seq, d = x.shape
    n_meta = meta_tokens.shape[0]
    pad = (-n_meta) % ATT_BLOCK
    tp = pad + n_meta + seq
    assert tp % ATT_BLOCK == 0 and tp % CHUNK == 0 and d % LANES == 0
    m = b * tp
    tm = 384 if m % 384 == 0 else 128
    tm_prep = 256 if m % 256 == 0 else 128

    def row(vec):
        return vec.reshape(1, d).astype(F32)

    meta = jnp.broadcast_to(meta_tokens[None].astype(x.dtype), (b, n_meta, d))
    h = jnp.concatenate([jnp.zeros((b, pad, d), x.dtype), meta, x], axis=1).reshape(m, d)

    qkv, v_first = _qkv_call(h, row(attn_norm[0]), w_qkv[0].astype(BF16), tm)
    o = _attn_call(qkv.reshape(b, tp, 3 * d), pad, 4 * LANES).reshape(m, d)
    h = _mix_ffn_call(h, o, w_o_attn[0].astype(BF16), row(ffn_norm[0]),
                      w_ffn_in[0].astype(BF16), w_ffn_out[0].astype(BF16), None, tm)

    vecs = jnp.concatenate([row(w0[0]), row(a0[0]), row(v0[0]), row(k_k[0]), row(k_a[0]),
                            jnp.zeros((3, d), F32)], axis=0)
    loras = [w.astype(BF16) for w in (w1[0], w2[0], a1[0], a2[0], v1[0], v2[0], g1[0], g2[0])]
    seqs = _rwkv_prep_call(h, v_first, row(rwkv_norm[0]), rwkv_mu[0], vecs,
                           w_rkv[0].astype(BF16), loras, tm_prep, tp)
    y = _rwkv_scan_call([s.reshape(b, tp, d) for s in seqs], row(r_k[0]), row(ln_x_w[0]),
                        row(ln_x_b[0]), 8 * LANES, 1).reshape(m, d)
    h = _mix_ffn_call(h, y, w_o_rwkv[0].astype(BF16), row(ffn_norm[1]),
                      w_ffn_in[1].astype(BF16), w_ffn_out[1].astype(BF16),
                      row(final_norm), tm)
    return h.reshape(b, tp, d)[:, pad + n_meta:]
```

```python
import functools

import jax
import jax.numpy as jnp
from jax import lax
from jax.experimental import pallas as pl
from jax.experimental.pallas import tpu as pltpu

F32 = jnp.float32
BF16 = jnp.bfloat16

HEAD_DIM = 64
N_META = 16
ATT_BLOCK = 128
LANES = 128
CHUNK = 64
RMS_EPS = 1e-6
GN_EPS = 64e-5
L2_EPS = 1e-12
EXP_UNDERFLOW = -88.0
VMEM_LIMIT = 56 * 1024 * 1024

_NN = (((1,), (0,)), ((), ()))
_NT = (((1,), (1,)), ((), ()))
_TN = (((0,), (0,)), ((), ()))


def _dg(a, b, dn=_NN):
    return lax.dot_general(a, b, dn, preferred_element_type=F32)


def _split(x, n):
    parts = []
    for _ in range(n):
        p = x.astype(BF16)
        parts.append(p)
        x = x - p.astype(F32)
    return parts


def _mm(a, b, dn=_NN, passes=3):
    if passes == 1:
        return _dg(a.astype(BF16), b.astype(BF16), dn)
    ah, al = _split(a, 2)
    bh, bl = _split(b, 2)
    return _dg(ah, bh, dn) + (_dg(ah, bl, dn) + _dg(al, bh, dn))


def _mm_exact_lhs(a_bf, b, dn=_NN, n=3):
    parts = _split(b, n)
    out = _dg(a_bf, parts[-1], dn)
    for p in parts[-2::-1]:
        out = out + _dg(a_bf, p, dn)
    return out


def _mm_exact_rhs(a, b_bf, dn=_NN, n=3):
    parts = _split(a, n)
    out = _dg(parts[-1], b_bf, dn)
    for p in parts[-2::-1]:
        out = out + _dg(p, b_bf, dn)
    return out


def _rms(x, g):
    return x * lax.rsqrt(jnp.mean(x * x, axis=-1, keepdims=True) + RMS_EPS) * g


def _softplus_neg_abs(z):
    return jnp.log(1.0 + jnp.exp(-jnp.abs(z)))


def _sigmoid(x):
    return 1.0 / (1.0 + jnp.exp(-x))


def _const_spec(shape):
    nd = len(shape)
    return pl.BlockSpec(shape, lambda *_: (0,) * nd, pipeline_mode=pl.Buffered(1))


def _qkv_kernel(x_ref, g_ref, w_ref, qkv_ref, v_ref, *, d):
    xb = _rms(x_ref[...], g_ref[...]).astype(BF16)
    for j in range(3):
        y = _dg(xb, w_ref[:, j * d:(j + 1) * d])
        if j == 0:
            y = y * (HEAD_DIM ** -0.5)
        qkv_ref[:, j * d:(j + 1) * d] = y.astype(BF16)
        if j == 2:
            v_ref[...] = y


def _qkv_call(h2d, g, w_bf, tm):
    m, d = h2d.shape
    return pl.pallas_call(
        functools.partial(_qkv_kernel, d=d),
        grid=(m // tm,),
        in_specs=[pl.BlockSpec((tm, d), lambda i: (i, 0)),
                  _const_spec((1, d)),
                  _const_spec((d, 3 * d))],
        out_specs=[pl.BlockSpec((tm, 3 * d), lambda i: (i, 0)),
                   pl.BlockSpec((tm, d), lambda i: (i, 0))],
        out_shape=[jax.ShapeDtypeStruct((m, 3 * d), BF16),
                   jax.ShapeDtypeStruct((m, d), F32)],
        compiler_params=pltpu.CompilerParams(
            dimension_semantics=("parallel",), vmem_limit_bytes=VMEM_LIMIT),
        name="qkv",
    )(h2d, g, w_bf)


def _attn_kernel(q_ref, k_ref, v_ref, o_ref, *, pad):
    blk = ATT_BLOCK
    n_groups = q_ref.shape[1] // LANES
    i = pl.program_id(2)
    row = lax.broadcasted_iota(jnp.int32, (blk, blk), 0)
    col = lax.broadcasted_iota(jnp.int32, (blk, blk), 1)
    lane_lo = lax.broadcasted_iota(jnp.int32, (blk, LANES), 1) < HEAD_DIM
    q_heads = []
    for p in range(n_groups):
        q = q_ref[:, p * LANES:(p + 1) * LANES]
        zero_q = jnp.zeros_like(q)
        q_heads += [jnp.where(lane_lo, q, zero_q), jnp.where(lane_lo, zero_q, q)]
    upper = (row > col).astype(BF16)

    def cond(carry):
        jj, c_max = carry[0], carry[1]
        return (jj <= i) & (c_max > EXP_UNDERFLOW)

    def body(carry):
        jj, _, accs, cs = carry
        j = i - jj
        start = pl.multiple_of(j * blk, blk)
        s_idx = j * blk + col
        mask = (s_idx < i * blk + row) & (s_idx >= pad)
        heads = range(2 * n_groups)
        kts = [k_ref[pl.ds(start, blk), p * LANES:(p + 1) * LANES] for p in range(n_groups)]
        vts = [v_ref[pl.ds(start, blk), p * LANES:(p + 1) * LANES] for p in range(n_groups)]
        zs = [_dg(q_heads[h], kts[h // 2], _NT) for h in heads]
        sps = [_softplus_neg_abs(zs[h]) for h in heads]
        log_rest = [jnp.where(mask, -jnp.maximum(zs[h], 0.0) - sps[h], 0.0) for h in heads]
        later = [_mm_exact_rhs(log_rest[h], upper, n=2) for h in heads]
        att = [jnp.where(mask, jnp.exp(jnp.minimum(zs[h], 0.0) - sps[h] + later[h] + cs[h]), 0.0)
               for h in heads]
        outs = [_dg(att[h].astype(BF16), vts[h // 2]) for h in heads]
        new_cs = [cs[h] + jnp.sum(log_rest[h], axis=1, keepdims=True) for h in heads]
        new_accs = [accs[p] + jnp.where(lane_lo, outs[2 * p], outs[2 * p + 1])
                    for p in range(n_groups)]
        c_all = new_cs[0]
        for c in new_cs[1:]:
            c_all = jnp.maximum(c_all, c)
        return jj + 1, jnp.max(c_all), tuple(new_accs), tuple(new_cs)

    init = (jnp.int32(0), jnp.float32(0.0),
            tuple(jnp.zeros((blk, LANES), F32) for _ in range(n_groups)),
            tuple(jnp.zeros((blk, 1), F32) for _ in range(2 * n_groups)))
    accs = lax.while_loop(cond, body, init)[2]
    for p in range(n_groups):
        o_ref[:, p * LANES:(p + 1) * LANES] = accs[p].astype(BF16)


def _attn_call(qkv3d, pad, lane_set):
    b, tp, d3 = qkv3d.shape
    d = d3 // 3
    ng = d // lane_set
    blk = ATT_BLOCK
    return pl.pallas_call(
        functools.partial(_attn_kernel, pad=pad),
        grid=(b, ng, tp // blk),
        in_specs=[pl.BlockSpec((None, blk, lane_set), lambda bi, p, i: (bi, i, p)),
                  pl.BlockSpec((None, tp, lane_set), lambda bi, p, i: (bi, 0, ng + p)),
                  pl.BlockSpec((None, tp, lane_set), lambda bi, p, i: (bi, 0, 2 * ng + p))],
        out_specs=pl.BlockSpec((None, blk, lane_set), lambda bi, p, i: (bi, i, p)),
        out_shape=jax.ShapeDtypeStruct((b, tp, d), BF16),
        compiler_params=pltpu.CompilerParams(
            dimension_semantics=("parallel", "parallel", "arbitrary"),
            vmem_limit_bytes=VMEM_LIMIT),
        name="attention",
    )(qkv3d, qkv3d, qkv3d)


def _mix_ffn_body(h, o, wo_ref, g_ref, win_ref, wout_ref, f):
    h1 = h + _dg(o, wo_ref[...])
    xb = _rms(h1, g_ref[...]).astype(BF16)
    gu = _dg(xb, win_ref[...])
    gate = gu[:, :f]
    act = (gate * _sigmoid(gate) * gu[:, f:]).astype(BF16)
    return h1 + _dg(act, wout_ref[...])


def _mix_ffn_kernel(h_ref, o_ref, wo_ref, g_ref, win_ref, wout_ref, out_ref, *, f):
    out_ref[...] = _mix_ffn_body(h_ref[...], o_ref[...], wo_ref, g_ref, win_ref, wout_ref, f)


def _mix_ffn_final_kernel(*refs, f, pieces):
    h_refs, o_refs = refs[:pieces], refs[pieces:2 * pieces]
    wo_ref, g_ref, win_ref, wout_ref, fg_ref, out_ref = refs[2 * pieces:]
    h = jnp.concatenate([r[...] for r in h_refs], axis=0)
    o = jnp.concatenate([r[...] for r in o_refs], axis=0)
    out = _mix_ffn_body(h, o, wo_ref, g_ref, win_ref, wout_ref, f)
    out_ref[...] = _rms(out, fg_ref[...])


def _mix_ffn_call(h2d, o2d, wo_bf, g, win_bf, wout_bf, tm):
    m, d = h2d.shape
    f = wout_bf.shape[0]
    return pl.pallas_call(
        functools.partial(_mix_ffn_kernel, f=f),
        grid=(m // tm,),
        in_specs=[pl.BlockSpec((tm, d), lambda i: (i, 0)),
                  pl.BlockSpec((tm, d), lambda i: (i, 0)),
                  _const_spec((d, d)),
                  _const_spec((1, d)),
                  _const_spec((d, 2 * f)),
                  _const_spec((f, d))],
        out_specs=pl.BlockSpec((tm, d), lambda i: (i, 0)),
        out_shape=jax.ShapeDtypeStruct((m, d), F32),
        compiler_params=pltpu.CompilerParams(
            dimension_semantics=("parallel",), vmem_limit_bytes=VMEM_LIMIT),
        name="mix_ffn",
    )(h2d, o2d, wo_bf, g, win_bf, wout_bf)


def _mix_ffn_final_call(h3d, o3d, wo_bf, g, win_bf, wout_bf, final_g, skip, tm):
    b, tp, d = h3d.shape
    f = wout_bf.shape[0]
    blk = ATT_BLOCK
    pieces = tm // blk
    first = skip // blk

    def piece_spec(q):
        return pl.BlockSpec((None, blk, d), lambda bi, j: (bi, first + j * pieces + q, 0))

    in_specs = [piece_spec(q) for q in range(pieces)] * 2
    in_specs += [_const_spec((d, d)), _const_spec((1, d)), _const_spec((d, 2 * f)),
                 _const_spec((f, d)), _const_spec((1, d))]
    return pl.pallas_call(
        functools.partial(_mix_ffn_final_kernel, f=f, pieces=pieces),
        grid=(b, (tp - skip) // tm),
        in_specs=in_specs,
        out_specs=pl.BlockSpec((None, tm, d), lambda bi, j: (bi, j, 0)),
        out_shape=jax.ShapeDtypeStruct((b, tp - skip, d), F32),
        compiler_params=pltpu.CompilerParams(
            dimension_semantics=("parallel", "parallel"), vmem_limit_bytes=VMEM_LIMIT),
        name="mix_ffn_final",
    )(*([h3d] * pieces), *([o3d] * pieces), wo_bf, g, win_bf, wout_bf, final_g)


def _group_sum(x, lane_lo):
    lo = jnp.sum(jnp.where(lane_lo, x, 0.0), axis=1, keepdims=True)
    hi = jnp.sum(jnp.where(lane_lo, 0.0, x), axis=1, keepdims=True)
    return jnp.where(lane_lo, lo, hi)


def _rwkv_prep_kernel(h_ref, hprev_ref, vf_ref, g_ref, mu_ref, vec_ref, wrkv_ref,
                      w1_ref, w2_ref, a1_ref, a2_ref, v1_ref, v2_ref, g1_ref, g2_ref,
                      r_ref, k_ref, v_ref, lw_ref, kk_ref, b_ref, gate_ref, *, tm, tp):
    i = pl.program_id(0)
    g = g_ref[...]
    hn = _rms(h_ref[...], g)
    prev_last = _rms(hprev_ref[...], g)[7:8, :]
    local = lax.broadcasted_iota(jnp.int32, (tm, 1), 0)
    prev = jnp.where(local == 0, prev_last, pltpu.roll(hn, 1, axis=0))
    xx = jnp.where((i * tm + local) % tp == 0, 0.0, prev - hn)

    def mix(n):
        return (hn + xx * mu_ref[n:n + 1, :]).astype(BF16)

    w0, a0, v0 = vec_ref[0:1, :], vec_ref[1:2, :], vec_ref[2:3, :]
    k_k, k_a = vec_ref[3:4, :], vec_ref[4:5, :]

    r_ref[...] = _dg(mix(0), wrkv_ref[0])
    k = _dg(mix(1), wrkv_ref[1])
    xv = mix(2)
    v = _dg(xv, wrkv_ref[2])

    wl = w0 + _dg(jnp.tanh(_dg(mix(3), w1_ref[...])).astype(BF16), w2_ref[...])
    w_log = -(jnp.maximum(-wl, 0.0) + _softplus_neg_abs(wl)) - 0.5
    lw_ref[...] = -jnp.exp(w_log)

    v_gate = _sigmoid(v0 + _dg(_dg(xv, v1_ref[...]).astype(BF16), v2_ref[...]))
    v_ref[...] = v + (vf_ref[...] - v) * v_gate
    a = _sigmoid(a0 + _dg(_dg(mix(4), a1_ref[...]).astype(BF16), a2_ref[...]))
    gate_ref[...] = _dg(_sigmoid(_dg(mix(5), g1_ref[...])).astype(BF16), g2_ref[...])

    lane_lo = lax.broadcasted_iota(jnp.int32, (tm, LANES), 1) < HEAD_DIM
    kk = k * k_k
    for p in range(kk.shape[1] // LANES):
        sl = slice(p * LANES, (p + 1) * LANES)
        kkp = kk[:, sl]
        norm = jnp.maximum(jnp.sqrt(_group_sum(kkp * kkp, lane_lo)), L2_EPS)
        kkp = kkp / norm
        kk_ref[:, sl] = kkp
        b_ref[:, sl] = kkp * a[:, sl]
    k_ref[...] = k * (1.0 + (a - 1.0) * k_a)


def _rwkv_prep_call(h2d, vf2d, g, mu, vecs, wrkv_bf, loras, tm, tp):
    m, d = h2d.shape
    row_spec = pl.BlockSpec((tm, d), lambda i: (i, 0))
    in_specs = [row_spec,
                pl.BlockSpec((8, d), lambda i: (jnp.maximum(i * (tm // 8) - 1, 0), 0)),
                row_spec,
                _const_spec((1, d)), _const_spec(mu.shape), _const_spec(vecs.shape),
                _const_spec(wrkv_bf.shape)]
    in_specs += [_const_spec(w.shape) for w in loras]
    return pl.pallas_call(
        functools.partial(_rwkv_prep_kernel, tm=tm, tp=tp),
        grid=(m // tm,),
        in_specs=in_specs,
        out_specs=[row_spec] * 7,
        out_shape=[jax.ShapeDtypeStruct((m, d), F32)] * 7,
        compiler_params=pltpu.CompilerParams(
            dimension_semantics=("parallel",), vmem_limit_bytes=VMEM_LIMIT),
        name="rwkv_prep",
    )(h2d, h2d, vf2d, g, mu, vecs, wrkv_bf, *loras)


def _stack_heads(x, lane_lo):
    zero = jnp.zeros_like(x)
    return jnp.concatenate([jnp.where(lane_lo, x, zero), jnp.where(lane_lo, zero, x)], axis=0)


def _rwkv_chunk(rs, ks, vs, lws, kks, bs, cums, states, passes):
    c = CHUNK
    n = 2 * c
    groups = range(len(rs))
    lane_lo = lax.broadcasted_iota(jnp.int32, (c, LANES), 1) < HEAD_DIM
    st = functools.partial(_stack_heads, lane_lo=lane_lo)
    row = lax.broadcasted_iota(jnp.int32, (n, n), 0)
    col = lax.broadcasted_iota(jnp.int32, (n, n), 1)
    strict = (col % c) < (row % c)
    incl = (col % c) <= (row % c)
    eye = (row == col).astype(F32)

    def mm(a, b, dn=_NN):
        return _mm(a, b, dn, passes)

    e_inv = [jnp.exp(-cums[p]) for p in groups]
    a_t = [-kks[p] * jnp.exp(cums[p] - lws[p]) for p in groups]
    r_t = [rs[p] * jnp.exp(cums[p]) for p in groups]
    s_v = [st(vs[p]) for p in groups]
    gram = [mm(jnp.concatenate([st(a_t[p]), st(r_t[p])], axis=0),
               jnp.concatenate([st(bs[p] * e_inv[p]), st(ks[p] * e_inv[p])], axis=0), _NT)
            for p in groups]
    x1 = [mm(jnp.concatenate([a_t[p], r_t[p]], axis=0), states[p], _NT) for p in groups]
    a_ab = [jnp.where(strict, gram[p][:n, :n], 0.0) for p in groups]
    rhs = [st(x1[p][:c]) + mm(jnp.where(strict, gram[p][:n, n:], 0.0), s_v[p]) for p in groups]

    inv = [eye + a_ab[p] for p in groups]
    power = a_ab
    for _ in range((c - 1).bit_length() - 1):
        power = [mm(power[p], power[p]) for p in groups]
        inv = [inv[p] + mm(inv[p], power[p]) for p in groups]

    s_uv = [jnp.concatenate([mm(inv[p], rhs[p]), s_v[p]], axis=0) for p in groups]
    incl2 = jnp.concatenate([incl, incl], axis=1)
    s_y = [st(x1[p][c:]) + mm(jnp.where(incl2, gram[p][n:, :], 0.0), s_uv[p]) for p in groups]
    ys = [s_y[p][:c] + s_y[p][c:] for p in groups]
    totals = [cums[p][c - 1:c, :] for p in groups]
    e_rem = [jnp.exp(totals[p] - cums[p]) for p in groups]
    new_states = [states[p] * jnp.exp(totals[p])
                  + mm(s_uv[p],
                       jnp.concatenate([st(bs[p] * e_rem[p]), st(ks[p] * e_rem[p])], axis=0), _TN)
                  for p in groups]
    return ys, new_states


def _rwkv_scan_kernel(r_ref, k_ref, v_ref, lw_ref, kk_ref, b_ref, gate_ref,
                      rk_ref, lnw_ref, lnb_ref, out_ref, state_ref, *, passes):
    c = CHUNK

    @pl.when(pl.program_id(2) == 0)
    def _():
        state_ref[...] = jnp.zeros_like(state_ref)

    lw = lw_ref[...]
    rowc = lax.broadcasted_iota(jnp.int32, (c, c), 0)
    colc = lax.broadcasted_iota(jnp.int32, (c, c), 1)
    cum = _mm_exact_lhs((colc <= rowc).astype(BF16), lw, n=3)
    groups = range(lw.shape[1] // LANES)
    sls = [slice(p * LANES, (p + 1) * LANES) for p in groups]
    rs = [r_ref[:, sl] for sl in sls]
    ks = [k_ref[:, sl] for sl in sls]
    vs = [v_ref[:, sl] for sl in sls]
    ys, new_states = _rwkv_chunk(rs, ks, vs, [lw[:, sl] for sl in sls],
                                 [kk_ref[:, sl] for sl in sls], [b_ref[:, sl] for sl in sls],
                                 [cum[:, sl] for sl in sls], [state_ref[p] for p in groups], passes)
    for p in groups:
        state_ref[p] = new_states[p]
    lane_lo = lax.broadcasted_iota(jnp.int32, (c, LANES), 1) < HEAD_DIM
    inv_n = 1.0 / HEAD_DIM
    means = [_group_sum(ys[p], lane_lo) * inv_n for p in groups]
    devs = [ys[p] - means[p] for p in groups]
    variances = [_group_sum(devs[p] * devs[p], lane_lo) * inv_n for p in groups]
    bonus = [_group_sum(rs[p] * ks[p] * rk_ref[:, sls[p]], lane_lo) * vs[p] for p in groups]
    for p in groups:
        sl = sls[p]
        yn = devs[p] * lax.rsqrt(variances[p] + GN_EPS) * lnw_ref[:, sl] + lnb_ref[:, sl]
        out_ref[:, sl] = ((yn + bonus[p]) * gate_ref[:, sl]).astype(BF16)


def _rwkv_scan_call(seqs, rk, lnw, lnb, lane_group, passes):
    b, tp, d = seqs[0].shape
    c = CHUNK
    seq_spec = pl.BlockSpec((None, c, lane_group), lambda bi, l, ci: (bi, ci, l))
    vec_spec = pl.BlockSpec((1, lane_group), lambda bi, l, ci: (0, l))
    return pl.pallas_call(
        functools.partial(_rwkv_scan_kernel, passes=passes),
        grid=(b, d // lane_group, tp // c),
        in_specs=[seq_spec] * 7 + [vec_spec] * 3,
        out_specs=seq_spec,
        out_shape=jax.ShapeDtypeStruct((b, tp, d), BF16),
        scratch_shapes=[pltpu.VMEM((lane_group // LANES, LANES, LANES), F32)],
        compiler_params=pltpu.CompilerParams(
            dimension_semantics=("parallel", "parallel", "arbitrary"),
            vmem_limit_bytes=VMEM_LIMIT),
        name="rwkv_scan",
    )(*seqs, rk, lnw, lnb)


def kernel(x, meta_tokens, attn_norm, w_qkv, w_o_attn, rwkv_norm, rwkv_mu, w_rkv, w_o_rwkv,
           w0, w1, w2, a0, a1, a2, v0, v1, v2, g1, g2, k_k, k_a, r_k, ln_x_w, ln_x_b,
           ffn_norm, w_ffn_in, w_ffn_out, final_norm):
    b, seq, d = x.shape
    n_meta = meta_tokens.shape[0]
    pad = (-n_meta) % ATT_BLOCK
    tp = pad + n_meta + seq
    assert tp % ATT_BLOCK == 0 and tp % CHUNK == 0 and d % LANES == 0
    m = b * tp
    tm = 384 if m % 384 == 0 else 128
    tm_prep = 256 if m % 256 == 0 else 128

    def row(vec):
        return vec.reshape(1, d).astype(F32)

    meta = jnp.broadcast_to(meta_tokens[None].astype(x.dtype), (b, n_meta, d))
    h = jnp.concatenate([jnp.zeros((b, pad, d), x.dtype), meta, x], axis=1).reshape(m, d)

    qkv, v_first = _qkv_call(h, row(attn_norm[0]), w_qkv[0].astype(BF16), tm)
    o = _attn_call(qkv.reshape(b, tp, 3 * d), pad, 4 * LANES).reshape(m, d)
    h = _mix_ffn_call(h, o, w_o_attn[0].astype(BF16), row(ffn_norm[0]),
                      w_ffn_in[0].astype(BF16), w_ffn_out[0].astype(BF16), tm)

    vecs = jnp.concatenate([row(w0[0]), row(a0[0]), row(v0[0]), row(k_k[0]), row(k_a[0]),
                            jnp.zeros((3, d), F32)], axis=0)
    loras = [w.astype(BF16) for w in (w1[0], w2[0], a1[0], a2[0], v1[0], v2[0], g1[0], g2[0])]
    seqs = _rwkv_prep_call(h, v_first, row(rwkv_norm[0]), rwkv_mu[0], vecs,
                           w_rkv[0].astype(BF16), loras, tm_prep, tp)
    y = _rwkv_scan_call([s.reshape(b, tp, d) for s in seqs], row(r_k[0]), row(ln_x_w[0]),
                        row(ln_x_b[0]), 8 * LANES, 1)
    tm_out = 512 if seq % 512 == 0 else ATT_BLOCK
    return _mix_ffn_final_call(h.reshape(b, tp, d), y, w_o_rwkv[0].astype(BF16),
                               row(ffn_norm[1]), w_ffn_in[1].astype(BF16),
                               w_ffn_out[1].astype(BF16), row(final_norm),
                               pad + n_meta, tm_out)
```

```python
import functools

import jax
import jax.numpy as jnp
from jax import lax
from jax.experimental import pallas as pl
from jax.experimental.pallas import tpu as pltpu

F32 = jnp.float32
BF16 = jnp.bfloat16

HEAD_DIM = 64
N_META = 16
ATT_BLOCK = 128
KEY_BLOCKS_PER_TRIP = 2
LANES = 128
CHUNK = 64
RMS_EPS = 1e-6
GN_EPS = 64e-5
L2_EPS = 1e-12
EXP_UNDERFLOW = -88.0
VMEM_LIMIT = 56 * 1024 * 1024

_NN = (((1,), (0,)), ((), ()))
_NT = (((1,), (1,)), ((), ()))
_TN = (((0,), (0,)), ((), ()))


def _dg(a, b, dn=_NN):
    return lax.dot_general(a, b, dn, preferred_element_type=F32)


def _split(x, n):
    parts = []
    for _ in range(n):
        p = x.astype(BF16)
        parts.append(p)
        x = x - p.astype(F32)
    return parts


def _mm(a, b, dn=_NN, passes=3):
    if passes == 1:
        return _dg(a.astype(BF16), b.astype(BF16), dn)
    ah, al = _split(a, 2)
    bh, bl = _split(b, 2)
    return _dg(ah, bh, dn) + (_dg(ah, bl, dn) + _dg(al, bh, dn))


def _mm_exact_lhs(a_bf, b, dn=_NN, n=3):
    parts = _split(b, n)
    out = _dg(a_bf, parts[-1], dn)
    for p in parts[-2::-1]:
        out = out + _dg(a_bf, p, dn)
    return out


def _mm_exact_rhs(a, b_bf, dn=_NN, n=3):
    parts = _split(a, n)
    out = _dg(parts[-1], b_bf, dn)
    for p in parts[-2::-1]:
        out = out + _dg(p, b_bf, dn)
    return out


def _rms(x, g):
    return x * lax.rsqrt(jnp.mean(x * x, axis=-1, keepdims=True) + RMS_EPS) * g


def _softplus_neg_abs(z):
    return jnp.log(1.0 + jnp.exp(-jnp.abs(z)))


def _sigmoid(x):
    return 1.0 / (1.0 + jnp.exp(-x))


def _const_spec(shape):
    nd = len(shape)
    return pl.BlockSpec(shape, lambda *_: (0,) * nd, pipeline_mode=pl.Buffered(1))


def _qkv_kernel(x_ref, g_ref, w_ref, qkv_ref, v_ref, *, d):
    xb = _rms(x_ref[...], g_ref[...]).astype(BF16)
    for j in range(3):
        y = _dg(xb, w_ref[:, j * d:(j + 1) * d])
        if j == 0:
            y = y * (HEAD_DIM ** -0.5)
        qkv_ref[:, j * d:(j + 1) * d] = y.astype(BF16)
        if j == 2:
            v_ref[...] = y


def _qkv_call(h2d, g, w_bf, tm):
    m, d = h2d.shape
    return pl.pallas_call(
        functools.partial(_qkv_kernel, d=d),
        grid=(m // tm,),
        in_specs=[pl.BlockSpec((tm, d), lambda i: (i, 0)),
                  _const_spec((1, d)),
                  _const_spec((d, 3 * d))],
        out_specs=[pl.BlockSpec((tm, 3 * d), lambda i: (i, 0)),
                   pl.BlockSpec((tm, d), lambda i: (i, 0))],
        out_shape=[jax.ShapeDtypeStruct((m, 3 * d), BF16),
                   jax.ShapeDtypeStruct((m, d), F32)],
        compiler_params=pltpu.CompilerParams(
            dimension_semantics=("parallel",), vmem_limit_bytes=VMEM_LIMIT),
        name="qkv",
    )(h2d, g, w_bf)


def _attn_kernel(q_ref, k_ref, v_ref, o_ref, *, pad):
    blk = ATT_BLOCK
    n_groups = q_ref.shape[1] // LANES
    i = pl.program_id(2)
    row = lax.broadcasted_iota(jnp.int32, (blk, blk), 0)
    col = lax.broadcasted_iota(jnp.int32, (blk, blk), 1)
    lane_lo = lax.broadcasted_iota(jnp.int32, (blk, LANES), 1) < HEAD_DIM
    q_heads = []
    for p in range(n_groups):
        q = q_ref[:, p * LANES:(p + 1) * LANES]
        zero_q = jnp.zeros_like(q)
        q_heads += [jnp.where(lane_lo, q, zero_q), jnp.where(lane_lo, zero_q, q)]
    upper = (row > col).astype(BF16)

    def cond(carry):
        jj, c_max = carry[0], carry[1]
        return (jj <= i) & (c_max > EXP_UNDERFLOW)

    def body(carry):
        jj, _, accs, cs = carry
        n_heads = 2 * n_groups
        masks, kts, vts = [], [], []
        for w in range(KEY_BLOCKS_PER_TRIP):
            j = i - jj - w
            start = pl.multiple_of(jnp.maximum(j, 0) * blk, blk)
            s_idx = j * blk + col
            masks.append((s_idx < i * blk + row) & (s_idx >= pad))
            kts.append([k_ref[pl.ds(start, blk), p * LANES:(p + 1) * LANES] for p in range(n_groups)])
            vts.append([v_ref[pl.ds(start, blk), p * LANES:(p + 1) * LANES] for p in range(n_groups)])
        units = [(w, h) for w in range(KEY_BLOCKS_PER_TRIP) for h in range(n_heads)]
        zs = [_dg(q_heads[h], kts[w][h // 2], _NT) for w, h in units]
        sps = [_softplus_neg_abs(z) for z in zs]
        log_rest = [jnp.where(masks[w], -jnp.maximum(zs[u], 0.0) - sps[u], 0.0)
                    for u, (w, h) in enumerate(units)]
        suffix = [_mm_exact_rhs(lr, upper, n=2) for lr in log_rest]
        row_tot = [jnp.sum(lr, axis=1, keepdims=True) for lr in log_rest]
        carried = list(cs)
        outs = []
        for u, (w, h) in enumerate(units):
            att = jnp.where(masks[w], jnp.exp(jnp.minimum(zs[u], 0.0) - sps[u] + suffix[u]
                                              + carried[h]), 0.0)
            outs.append(_dg(att.astype(BF16), vts[w][h // 2]))
            carried[h] = carried[h] + row_tot[u]
        new_accs = list(accs)
        for w in range(KEY_BLOCKS_PER_TRIP):
            for p in range(n_groups):
                new_accs[p] = new_accs[p] + jnp.where(lane_lo, outs[w * n_heads + 2 * p],
                                                      outs[w * n_heads + 2 * p + 1])
        c_all = carried[0]
        for c in carried[1:]:
            c_all = jnp.maximum(c_all, c)
        return jj + KEY_BLOCKS_PER_TRIP, jnp.max(c_all), tuple(new_accs), tuple(carried)

    init = (jnp.int32(0), jnp.float32(0.0),
            tuple(jnp.zeros((blk, LANES), F32) for _ in range(n_groups)),
            tuple(jnp.zeros((blk, 1), F32) for _ in range(2 * n_groups)))
    accs = lax.while_loop(cond, body, init)[2]
    for p in range(n_groups):
        o_ref[:, p * LANES:(p + 1) * LANES] = accs[p].astype(BF16)


def _attn_call(qkv3d, pad, lane_set):
    b, tp, d3 = qkv3d.shape
    d = d3 // 3
    ng = d // lane_set
    blk = ATT_BLOCK
    return pl.pallas_call(
        functools.partial(_attn_kernel, pad=pad),
        grid=(b, ng, tp // blk),
        in_specs=[pl.BlockSpec((None, blk, lane_set), lambda bi, p, i: (bi, i, p)),
                  pl.BlockSpec((None, tp, lane_set), lambda bi, p, i: (bi, 0, ng + p)),
                  pl.BlockSpec((None, tp, lane_set), lambda bi, p, i: (bi, 0, 2 * ng + p))],
        out_specs=pl.BlockSpec((None, blk, lane_set), lambda bi, p, i: (bi, i, p)),
        out_shape=jax.ShapeDtypeStruct((b, tp, d), BF16),
        compiler_params=pltpu.CompilerParams(
            dimension_semantics=("parallel", "parallel", "arbitrary"),
            vmem_limit_bytes=VMEM_LIMIT),
        name="attention",
    )(qkv3d, qkv3d, qkv3d)


def _mix_ffn_body(h, o, wo_ref, g_ref, win_ref, wout_ref, f):
    h1 = h + _dg(o, wo_ref[...])
    xb = _rms(h1, g_ref[...]).astype(BF16)
    gu = _dg(xb, win_ref[...])
    gate = gu[:, :f]
    act = (gate * _sigmoid(gate) * gu[:, f:]).astype(BF16)
    return h1 + _dg(act, wout_ref[...])


def _mix_ffn_kernel(h_ref, o_ref, wo_ref, g_ref, win_ref, wout_ref, out_ref, *, f):
    out_ref[...] = _mix_ffn_body(h_ref[...], o_ref[...], wo_ref, g_ref, win_ref, wout_ref, f)


def _mix_ffn_final_kernel(*refs, f, pieces):
    h_refs, o_refs = refs[:pieces], refs[pieces:2 * pieces]
    wo_ref, g_ref, win_ref, wout_ref, fg_ref, out_ref = refs[2 * pieces:]
    h = jnp.concatenate([r[...] for r in h_refs], axis=0)
    o = jnp.concatenate([r[...] for r in o_refs], axis=0)
    out = _mix_ffn_body(h, o, wo_ref, g_ref, win_ref, wout_ref, f)
    out_ref[...] = _rms(out, fg_ref[...])


def _mix_ffn_call(h2d, o2d, wo_bf, g, win_bf, wout_bf, tm):
    m, d = h2d.shape
    f = wout_bf.shape[0]
    return pl.pallas_call(
        functools.partial(_mix_ffn_kernel, f=f),
        grid=(m // tm,),
        in_specs=[pl.BlockSpec((tm, d), lambda i: (i, 0)),
                  pl.BlockSpec((tm, d), lambda i: (i, 0)),
                  _const_spec((d, d)),
                  _const_spec((1, d)),
                  _const_spec((d, 2 * f)),
                  _const_spec((f, d))],
        out_specs=pl.BlockSpec((tm, d), lambda i: (i, 0)),
        out_shape=jax.ShapeDtypeStruct((m, d), F32),
        compiler_params=pltpu.CompilerParams(
            dimension_semantics=("parallel",), vmem_limit_bytes=VMEM_LIMIT),
        name="mix_ffn",
    )(h2d, o2d, wo_bf, g, win_bf, wout_bf)


def _mix_ffn_final_call(h3d, o3d, wo_bf, g, win_bf, wout_bf, final_g, skip, tm):
    b, tp, d = h3d.shape
    f = wout_bf.shape[0]
    blk = ATT_BLOCK
    pieces = tm // blk
    first = skip // blk

    def piece_spec(q):
        return pl.BlockSpec((None, blk, d), lambda bi, j: (bi, first + j * pieces + q, 0))

    in_specs = [piece_spec(q) for q in range(pieces)] * 2
    in_specs += [_const_spec((d, d)), _const_spec((1, d)), _const_spec((d, 2 * f)),
                 _const_spec((f, d)), _const_spec((1, d))]
    return pl.pallas_call(
        functools.partial(_mix_ffn_final_kernel, f=f, pieces=pieces),
        grid=(b, (tp - skip) // tm),
        in_specs=in_specs,
        out_specs=pl.BlockSpec((None, tm, d), lambda bi, j: (bi, j, 0)),
        out_shape=jax.ShapeDtypeStruct((b, tp - skip, d), F32),
        compiler_params=pltpu.CompilerParams(
            dimension_semantics=("parallel", "parallel"), vmem_limit_bytes=VMEM_LIMIT),
        name="mix_ffn_final",
    )(*([h3d] * pieces), *([o3d] * pieces), wo_bf, g, win_bf, wout_bf, final_g)


def _group_sum(x, lane_lo):
    lo = jnp.sum(jnp.where(lane_lo, x, 0.0), axis=1, keepdims=True)
    hi = jnp.sum(jnp.where(lane_lo, 0.0, x), axis=1, keepdims=True)
    return jnp.where(lane_lo, lo, hi)


def _rwkv_prep_kernel(h_ref, hprev_ref, vf_ref, g_ref, mu_ref, vec_ref, wrkv_ref,
                      w1_ref, w2_ref, a1_ref, a2_ref, v1_ref, v2_ref, g1_ref, g2_ref,
                      r_ref, k_ref, v_ref, lw_ref, kk_ref, b_ref, gate_ref, *, tm, tp):
    i = pl.program_id(0)
    g = g_ref[...]
    hn = _rms(h_ref[...], g)
    prev_last = _rms(hprev_ref[...], g)[7:8, :]
    local = lax.broadcasted_iota(jnp.int32, (tm, 1), 0)
    prev = jnp.where(local == 0, prev_last, pltpu.roll(hn, 1, axis=0))
    xx = jnp.where((i * tm + local) % tp == 0, 0.0, prev - hn)

    def mix(n):
        return (hn + xx * mu_ref[n:n + 1, :]).astype(BF16)

    w0, a0, v0 = vec_ref[0:1, :], vec_ref[1:2, :], vec_ref[2:3, :]
    k_k, k_a = vec_ref[3:4, :], vec_ref[4:5, :]

    r_ref[...] = _dg(mix(0), wrkv_ref[0])
    k = _dg(mix(1), wrkv_ref[1])
    xv = mix(2)
    v = _dg(xv, wrkv_ref[2])

    wl = w0 + _dg(jnp.tanh(_dg(mix(3), w1_ref[...])).astype(BF16), w2_ref[...])
    w_log = -(jnp.maximum(-wl, 0.0) + _softplus_neg_abs(wl)) - 0.5
    lw_ref[...] = -jnp.exp(w_log)

    v_gate = _sigmoid(v0 + _dg(_dg(xv, v1_ref[...]).astype(BF16), v2_ref[...]))
    v_ref[...] = v + (vf_ref[...] - v) * v_gate
    a = _sigmoid(a0 + _dg(_dg(mix(4), a1_ref[...]).astype(BF16), a2_ref[...]))
    gate_ref[...] = _dg(_sigmoid(_dg(mix(5), g1_ref[...])).astype(BF16), g2_ref[...])

    lane_lo = lax.broadcasted_iota(jnp.int32, (tm, LANES), 1) < HEAD_DIM
    kk = k * k_k
    for p in range(kk.shape[1] // LANES):
        sl = slice(p * LANES, (p + 1) * LANES)
        kkp = kk[:, sl]
        norm = jnp.maximum(jnp.sqrt(_group_sum(kkp * kkp, lane_lo)), L2_EPS)
        kkp = kkp / norm
        kk_ref[:, sl] = kkp
        b_ref[:, sl] = kkp * a[:, sl]
    k_ref[...] = k * (1.0 + (a - 1.0) * k_a)


def _rwkv_prep_call(h2d, vf2d, g, mu, vecs, wrkv_bf, loras, tm, tp):
    m, d = h2d.shape
    row_spec = pl.BlockSpec((tm, d), lambda i: (i, 0))
    in_specs = [row_spec,
                pl.BlockSpec((8, d), lambda i: (jnp.maximum(i * (tm // 8) - 1, 0), 0)),
                row_spec,
                _const_spec((1, d)), _const_spec(mu.shape), _const_spec(vecs.shape),
                _const_spec(wrkv_bf.shape)]
    in_specs += [_const_spec(w.shape) for w in loras]
    return pl.pallas_call(
        functools.partial(_rwkv_prep_kernel, tm=tm, tp=tp),
        grid=(m // tm,),
        in_specs=in_specs,
        out_specs=[row_spec] * 7,
        out_shape=[jax.ShapeDtypeStruct((m, d), F32)] * 7,
        compiler_params=pltpu.CompilerParams(
            dimension_semantics=("parallel",), vmem_limit_bytes=VMEM_LIMIT),
        name="rwkv_prep",
    )(h2d, h2d, vf2d, g, mu, vecs, wrkv_bf, *loras)


def _stack_heads(x, lane_lo):
    zero = jnp.zeros_like(x)
    return jnp.concatenate([jnp.where(lane_lo, x, zero), jnp.where(lane_lo, zero, x)], axis=0)


def _rwkv_chunk_setup(rs, ks, vs, lws, kks, bs, cums, passes):
    c = CHUNK
    n = 2 * c
    units = range(len(rs))
    lane_lo = lax.broadcasted_iota(jnp.int32, (c, LANES), 1) < HEAD_DIM
    st = functools.partial(_stack_heads, lane_lo=lane_lo)
    row = lax.broadcasted_iota(jnp.int32, (n, n), 0)
    col = lax.broadcasted_iota(jnp.int32, (n, n), 1)
    strict = (col % c) < (row % c)
    incl = (col % c) <= (row % c)
    incl2 = jnp.concatenate([incl, incl], axis=1)
    eye = (row == col).astype(F32)

    def mm(a, b, dn=_NN):
        return _mm(a, b, dn, passes)

    e_inv = [jnp.exp(-cums[u]) for u in units]
    a_t = [-kks[u] * jnp.exp(cums[u] - lws[u]) for u in units]
    r_t = [rs[u] * jnp.exp(cums[u]) for u in units]
    s_v = [st(vs[u]) for u in units]
    gram = [mm(jnp.concatenate([st(a_t[u]), st(r_t[u])], axis=0),
               jnp.concatenate([st(bs[u] * e_inv[u]), st(ks[u] * e_inv[u])], axis=0), _NT)
            for u in units]
    a_ab = [jnp.where(strict, gram[u][:n, :n], 0.0) for u in units]
    ak_v = [mm(jnp.where(strict, gram[u][:n, n:], 0.0), s_v[u]) for u in units]

    inv = [eye + a_ab[u] for u in units]
    power = a_ab
    for _ in range((c - 1).bit_length() - 1):
        power = [mm(power[u], power[u]) for u in units]
        inv = [inv[u] + mm(inv[u], power[u]) for u in units]

    totals = [cums[u][c - 1:c, :] for u in units]
    e_rem = [jnp.exp(totals[u] - cums[u]) for u in units]
    return dict(
        ar=[jnp.concatenate([a_t[u], r_t[u]], axis=0) for u in units],
        s_v=s_v, ak_v=ak_v, inv=inv,
        m_r=[jnp.where(incl2, gram[u][n:, :], 0.0) for u in units],
        bk_rem=[jnp.concatenate([st(bs[u] * e_rem[u]), st(ks[u] * e_rem[u])], axis=0)
                for u in units],
        decay=[jnp.exp(totals[u]) for u in units])


def _rwkv_chunk_apply(setup, units, states, passes):
    c = CHUNK
    lane_lo = lax.broadcasted_iota(jnp.int32, (c, LANES), 1) < HEAD_DIM
    st = functools.partial(_stack_heads, lane_lo=lane_lo)
    idx = range(len(units))

    def mm(a, b, dn=_NN):
        return _mm(a, b, dn, passes)

    x1 = [mm(setup["ar"][units[i]], states[i], _NT) for i in idx]
    rhs = [st(x1[i][:c]) + setup["ak_v"][units[i]] for i in idx]
    s_uv = [jnp.concatenate([mm(setup["inv"][units[i]], rhs[i]), setup["s_v"][units[i]]], axis=0)
            for i in idx]
    s_y = [st(x1[i][c:]) + mm(setup["m_r"][units[i]], s_uv[i]) for i in idx]
    ys = [s_y[i][:c] + s_y[i][c:] for i in idx]
    new_states = [states[i] * setup["decay"][units[i]]
                  + mm(s_uv[i], setup["bk_rem"][units[i]], _TN) for i in idx]
    return ys, new_states


def _rwkv_scan_kernel(r_ref, k_ref, v_ref, lw_ref, kk_ref, b_ref, gate_ref,
                      rk_ref, lnw_ref, lnb_ref, out_ref, state_ref, *, passes):
    c = CHUNK
    rows = lw_ref.shape[0]
    n_chunks = rows // c
    n_groups = lw_ref.shape[1] // LANES

    @pl.when(pl.program_id(2) == 0)
    def _():
        state_ref[...] = jnp.zeros_like(state_ref)

    lw = lw_ref[...]
    rowc = lax.broadcasted_iota(jnp.int32, (rows, rows), 0)
    colc = lax.broadcasted_iota(jnp.int32, (rows, rows), 1)
    cum = _mm_exact_lhs(((colc <= rowc) & (colc // c == rowc // c)).astype(BF16), lw, n=3)

    windows = [(slice(ch * c, (ch + 1) * c), slice(p * LANES, (p + 1) * LANES))
               for ch in range(n_chunks) for p in range(n_groups)]
    units = range(len(windows))
    rs = [r_ref[w] for w in windows]
    ks = [k_ref[w] for w in windows]
    vs = [v_ref[w] for w in windows]
    setup = _rwkv_chunk_setup(rs, ks, vs, [lw[w] for w in windows], [kk_ref[w] for w in windows],
                              [b_ref[w] for w in windows], [cum[w] for w in windows], passes)
    states = [state_ref[p] for p in range(n_groups)]
    ys = []
    for ch in range(n_chunks):
        y_ch, states = _rwkv_chunk_apply(
            setup, [ch * n_groups + p for p in range(n_groups)], states, passes)
        ys += y_ch
    for p in range(n_groups):
        state_ref[p] = states[p]

    lane_lo = lax.broadcasted_iota(jnp.int32, (c, LANES), 1) < HEAD_DIM
    inv_n = 1.0 / HEAD_DIM
    means = [_group_sum(ys[u], lane_lo) * inv_n for u in units]
    devs = [ys[u] - means[u] for u in units]
    variances = [_group_sum(devs[u] * devs[u], lane_lo) * inv_n for u in units]
    bonus = [_group_sum(rs[u] * ks[u] * rk_ref[:, windows[u][1]], lane_lo) * vs[u] for u in units]
    for u in units:
        w = windows[u]
        yn = devs[u] * lax.rsqrt(variances[u] + GN_EPS) * lnw_ref[:, w[1]] + lnb_ref[:, w[1]]
        out_ref[w] = ((yn + bonus[u]) * gate_ref[w]).astype(BF16)


def _rwkv_scan_call(seqs, rk, lnw, lnb, lane_group, chunks_per_step, passes):
    b, tp, d = seqs[0].shape
    c = CHUNK * chunks_per_step
    seq_spec = pl.BlockSpec((None, c, lane_group), lambda bi, l, ci: (bi, ci, l))
    vec_spec = pl.BlockSpec((1, lane_group), lambda bi, l, ci: (0, l))
    return pl.pallas_call(
        functools.partial(_rwkv_scan_kernel, passes=passes),
        grid=(b, d // lane_group, tp // c),
        in_specs=[seq_spec] * 7 + [vec_spec] * 3,
        out_specs=seq_spec,
        out_shape=jax.ShapeDtypeStruct((b, tp, d), BF16),
        scratch_shapes=[pltpu.VMEM((lane_group // LANES, LANES, LANES), F32)],
        compiler_params=pltpu.CompilerParams(
            dimension_semantics=("parallel", "parallel", "arbitrary"),
            vmem_limit_bytes=VMEM_LIMIT),
        name="rwkv_scan",
    )(*seqs, rk, lnw, lnb)


def kernel(x, meta_tokens, attn_norm, w_qkv, w_o_attn, rwkv_norm, rwkv_mu, w_rkv, w_o_rwkv,
           w0, w1, w2, a0, a1, a2, v0, v1, v2, g1, g2, k_k, k_a, r_k, ln_x_w, ln_x_b,
           ffn_norm, w_ffn_in, w_ffn_out, final_norm):
    b, seq, d = x.shape
    n_meta = meta_tokens.shape[0]
    pad = (-n_meta) % ATT_BLOCK
    tp = pad + n_meta + seq
    assert tp % ATT_BLOCK == 0 and tp % CHUNK == 0 and d % LANES == 0
    m = b * tp
    tm = 384 if m % 384 == 0 else 128
    tm_prep = 256 if m % 256 == 0 else 128

    def row(vec):
        return vec.reshape(1, d).astype(F32)

    meta = jnp.broadcast_to(meta_tokens[None].astype(x.dtype), (b, n_meta, d))
    h = jnp.concatenate([jnp.zeros((b, pad, d), x.dtype), meta, x], axis=1).reshape(m, d)

    qkv, v_first = _qkv_call(h, row(attn_norm[0]), w_qkv[0].astype(BF16), tm)
    o = _attn_call(qkv.reshape(b, tp, 3 * d), pad, 4 * LANES).reshape(m, d)
    h = _mix_ffn_call(h, o, w_o_attn[0].astype(BF16), row(ffn_norm[0]),
                      w_ffn_in[0].astype(BF16), w_ffn_out[0].astype(BF16), tm)

    vecs = jnp.concatenate([row(w0[0]), row(a0[0]), row(v0[0]), row(k_k[0]), row(k_a[0]),
                            jnp.zeros((3, d), F32)], axis=0)
    loras = [w.astype(BF16) for w in (w1[0], w2[0], a1[0], a2[0], v1[0], v2[0], g1[0], g2[0])]
    seqs = _rwkv_prep_call(h, v_first, row(rwkv_norm[0]), rwkv_mu[0], vecs,
                           w_rkv[0].astype(BF16), loras, tm_prep, tp)
    chunks_per_step = 3 if (tp // CHUNK) % 3 == 0 else 2
    y = _rwkv_scan_call([s.reshape(b, tp, d) for s in seqs], row(r_k[0]), row(ln_x_w[0]),
                        row(ln_x_b[0]), 8 * LANES, chunks_per_step, 1)
    tm_out = 512 if seq % 512 == 0 else ATT_BLOCK
    return _mix_ffn_final_call(h.reshape(b, tp, d), y, w_o_rwkv[0].astype(BF16),
                               row(ffn_norm[1]), w_ffn_in[1].astype(BF16),
                               w_ffn_out[1].astype(BF16), row(final_norm),
                               pad + n_meta, tm_out)
```

```python
import functools

import jax
import jax.numpy as jnp
from jax import lax
from jax.experimental import pallas as pl
from jax.experimental.pallas import tpu as pltpu

F32 = jnp.float32
BF16 = jnp.bfloat16

HEAD_DIM = 64
N_META = 16
ATT_BLOCK = 128
KEY_BLOCKS_PER_TRIP = 2
LANES = 128
CHUNK = 64
RMS_EPS = 1e-6
GN_EPS = 64e-5
L2_EPS = 1e-12
EXP_UNDERFLOW = -88.0
VMEM_LIMIT = 56 * 1024 * 1024

_NN = (((1,), (0,)), ((), ()))
_NT = (((1,), (1,)), ((), ()))
_TN = (((0,), (0,)), ((), ()))


def _dg(a, b, dn=_NN):
    return lax.dot_general(a, b, dn, preferred_element_type=F32)


def _split(x, n):
    parts = []
    for _ in range(n):
        p = x.astype(BF16)
        parts.append(p)
        x = x - p.astype(F32)
    return parts


def _mm(a, b, dn=_NN, passes=3):
    if passes == 1:
        return _dg(a.astype(BF16), b.astype(BF16), dn)
    ah, al = _split(a, 2)
    bh, bl = _split(b, 2)
    return _dg(ah, bh, dn) + (_dg(ah, bl, dn) + _dg(al, bh, dn))


def _mm_exact_lhs(a_bf, b, dn=_NN, n=3):
    parts = _split(b, n)
    out = _dg(a_bf, parts[-1], dn)
    for p in parts[-2::-1]:
        out = out + _dg(a_bf, p, dn)
    return out


def _mm_exact_rhs(a, b_bf, dn=_NN, n=3):
    parts = _split(a, n)
    out = _dg(parts[-1], b_bf, dn)
    for p in parts[-2::-1]:
        out = out + _dg(p, b_bf, dn)
    return out


def _rms(x, g):
    return x * lax.rsqrt(jnp.mean(x * x, axis=-1, keepdims=True) + RMS_EPS) * g


def _softplus_neg_abs(z):
    return jnp.log(1.0 + jnp.exp(-jnp.abs(z)))


def _sigmoid(x):
    return 1.0 / (1.0 + jnp.exp(-x))


def _const_spec(shape):
    nd = len(shape)
    return pl.BlockSpec(shape, lambda *_: (0,) * nd, pipeline_mode=pl.Buffered(1))


def _padded_piece_specs(pieces, blocks_per_seq, d):
    def spec(q):
        def index_map(i):
            blk_id = i * pieces + q
            return blk_id // blocks_per_seq, jnp.maximum(blk_id % blocks_per_seq - 1, 0), 0
        return pl.BlockSpec((None, ATT_BLOCK, d), index_map)
    return [spec(q) for q in range(pieces)]


def _padded_tile(piece_refs, lead_ref, blocks_per_seq):
    i = pl.program_id(0)
    pieces = len(piece_refs)
    parts = [jnp.where((i * pieces + q) % blocks_per_seq == 0, lead_ref[...], r[...])
             for q, r in enumerate(piece_refs)]
    return jnp.concatenate(parts, axis=0)


def _qkv_kernel(*refs, d, pieces, blocks_per_seq):
    lead_ref, g_ref, w_ref, qkv_ref, v_ref = refs[pieces:]
    x = _padded_tile(refs[:pieces], lead_ref, blocks_per_seq)
    xb = _rms(x, g_ref[...]).astype(BF16)
    for j in range(3):
        y = _dg(xb, w_ref[:, j * d:(j + 1) * d])
        if j == 0:
            y = y * (HEAD_DIM ** -0.5)
        qkv_ref[:, j * d:(j + 1) * d] = y.astype(BF16)
        if j == 2:
            v_ref[...] = y


def _qkv_call(x, lead, g, w_bf, tm, blocks_per_seq):
    b, _, d = x.shape
    m = b * blocks_per_seq * ATT_BLOCK
    pieces = tm // ATT_BLOCK
    return pl.pallas_call(
        functools.partial(_qkv_kernel, d=d, pieces=pieces, blocks_per_seq=blocks_per_seq),
        grid=(m // tm,),
        in_specs=_padded_piece_specs(pieces, blocks_per_seq, d) + [
            _const_spec((ATT_BLOCK, d)), _const_spec((1, d)), _const_spec((d, 3 * d))],
        out_specs=[pl.BlockSpec((tm, 3 * d), lambda i: (i, 0)),
                   pl.BlockSpec((tm, d), lambda i: (i, 0))],
        out_shape=[jax.ShapeDtypeStruct((m, 3 * d), BF16),
                   jax.ShapeDtypeStruct((m, d), F32)],
        compiler_params=pltpu.CompilerParams(
            dimension_semantics=("parallel",), vmem_limit_bytes=VMEM_LIMIT),
        name="qkv",
    )(*([x] * pieces), lead, g, w_bf)


def _attn_kernel(q_ref, k_ref, v_ref, o_ref, *, pad):
    blk = ATT_BLOCK
    n_groups = q_ref.shape[1] // LANES
    i = pl.program_id(2)
    row = lax.broadcasted_iota(jnp.int32, (blk, blk), 0)
    col = lax.broadcasted_iota(jnp.int32, (blk, blk), 1)
    lane_lo = lax.broadcasted_iota(jnp.int32, (blk, LANES), 1) < HEAD_DIM
    q_heads = []
    for p in range(n_groups):
        q = q_ref[:, p * LANES:(p + 1) * LANES]
        zero_q = jnp.zeros_like(q)
        q_heads += [jnp.where(lane_lo, q, zero_q), jnp.where(lane_lo, zero_q, q)]
    upper = (row > col).astype(BF16)

    def cond(carry):
        jj, c_max = carry[0], carry[1]
        return (jj <= i) & (c_max > EXP_UNDERFLOW)

    def body(carry):
        jj, _, accs, cs = carry
        n_heads = 2 * n_groups
        masks, kts, vts = [], [], []
        for w in range(KEY_BLOCKS_PER_TRIP):
            j = i - jj - w
            start = pl.multiple_of(jnp.maximum(j, 0) * blk, blk)
            s_idx = j * blk + col
            masks.append((s_idx < i * blk + row) & (s_idx >= pad))
            kts.append([k_ref[pl.ds(start, blk), p * LANES:(p + 1) * LANES] for p in range(n_groups)])
            vts.append([v_ref[pl.ds(start, blk), p * LANES:(p + 1) * LANES] for p in range(n_groups)])
        units = [(w, h) for w in range(KEY_BLOCKS_PER_TRIP) for h in range(n_heads)]
        zs = [_dg(q_heads[h], kts[w][h // 2], _NT) for w, h in units]
        log_beta = [jnp.minimum(z, 0.0) - _softplus_neg_abs(z) for z in zs]
        log_rest = [jnp.where(masks[w], log_beta[u] - zs[u], 0.0)
                    for u, (w, h) in enumerate(units)]
        suffix = [_mm_exact_rhs(lr, upper, n=2) for lr in log_rest]
        row_tot = [jnp.sum(lr, axis=1, keepdims=True) for lr in log_rest]
        carried = list(cs)
        outs = []
        for u, (w, h) in enumerate(units):
            att = jnp.where(masks[w], jnp.exp(log_beta[u] + suffix[u] + carried[h]), 0.0)
            outs.append(_dg(att.astype(BF16), vts[w][h // 2]))
            carried[h] = carried[h] + row_tot[u]
        new_accs = list(accs)
        for w in range(KEY_BLOCKS_PER_TRIP):
            for p in range(n_groups):
                new_accs[p] = new_accs[p] + jnp.where(lane_lo, outs[w * n_heads + 2 * p],
                                                      outs[w * n_heads + 2 * p + 1])
        c_all = carried[0]
        for c in carried[1:]:
            c_all = jnp.maximum(c_all, c)
        return jj + KEY_BLOCKS_PER_TRIP, jnp.max(c_all), tuple(new_accs), tuple(carried)

    init = (jnp.int32(0), jnp.float32(0.0),
            tuple(jnp.zeros((blk, LANES), F32) for _ in range(n_groups)),
            tuple(jnp.zeros((blk, 1), F32) for _ in range(2 * n_groups)))
    accs = lax.while_loop(cond, body, init)[2]
    for p in range(n_groups):
        o_ref[:, p * LANES:(p + 1) * LANES] = accs[p].astype(BF16)


def _attn_call(qkv3d, pad, lane_set):
    b, tp, d3 = qkv3d.shape
    d = d3 // 3
    ng = d // lane_set
    blk = ATT_BLOCK
    return pl.pallas_call(
        functools.partial(_attn_kernel, pad=pad),
        grid=(b, ng, tp // blk),
        in_specs=[pl.BlockSpec((None, blk, lane_set), lambda bi, p, i: (bi, i, p)),
                  pl.BlockSpec((None, tp, lane_set), lambda bi, p, i: (bi, 0, ng + p)),
                  pl.BlockSpec((None, tp, lane_set), lambda bi, p, i: (bi, 0, 2 * ng + p))],
        out_specs=pl.BlockSpec((None, blk, lane_set), lambda bi, p, i: (bi, i, p)),
        out_shape=jax.ShapeDtypeStruct((b, tp, d), BF16),
        compiler_params=pltpu.CompilerParams(
            dimension_semantics=("parallel", "parallel", "arbitrary"),
            vmem_limit_bytes=VMEM_LIMIT),
        name="attention",
    )(qkv3d, qkv3d, qkv3d)


def _mix_ffn_body(h, o, wo_ref, g_ref, win_ref, wout_ref, f):
    h1 = h + _dg(o, wo_ref[...])
    xb = _rms(h1, g_ref[...]).astype(BF16)
    gu = _dg(xb, win_ref[...])
    gate = gu[:, :f]
    act = (gate * _sigmoid(gate) * gu[:, f:]).astype(BF16)
    return h1 + _dg(act, wout_ref[...])


def _mix_ffn_kernel(*refs, f, pieces, blocks_per_seq):
    lead_ref, o_ref, wo_ref, g_ref, win_ref, wout_ref, out_ref = refs[pieces:]
    h = _padded_tile(refs[:pieces], lead_ref, blocks_per_seq)
    out_ref[...] = _mix_ffn_body(h, o_ref[...], wo_ref, g_ref, win_ref, wout_ref, f)


def _mix_ffn_final_kernel(*refs, f, pieces):
    h_refs, o_refs = refs[:pieces], refs[pieces:2 * pieces]
    wo_ref, g_ref, win_ref, wout_ref, fg_ref, out_ref = refs[2 * pieces:]
    h = jnp.concatenate([r[...] for r in h_refs], axis=0)
    o = jnp.concatenate([r[...] for r in o_refs], axis=0)
    out = _mix_ffn_body(h, o, wo_ref, g_ref, win_ref, wout_ref, f)
    out_ref[...] = _rms(out, fg_ref[...])


def _mix_ffn_call(x, lead, o2d, wo_bf, g, win_bf, wout_bf, tm, blocks_per_seq):
    m, d = o2d.shape
    f = wout_bf.shape[0]
    pieces = tm // ATT_BLOCK
    return pl.pallas_call(
        functools.partial(_mix_ffn_kernel, f=f, pieces=pieces, blocks_per_seq=blocks_per_seq),
        grid=(m // tm,),
        in_specs=_padded_piece_specs(pieces, blocks_per_seq, d) + [
            _const_spec((ATT_BLOCK, d)),
            pl.BlockSpec((tm, d), lambda i: (i, 0)),
            _const_spec((d, d)),
            _const_spec((1, d)),
            _const_spec((d, 2 * f)),
            _const_spec((f, d))],
        out_specs=pl.BlockSpec((tm, d), lambda i: (i, 0)),
        out_shape=jax.ShapeDtypeStruct((m, d), F32),
        compiler_params=pltpu.CompilerParams(
            dimension_semantics=("parallel",), vmem_limit_bytes=VMEM_LIMIT),
        name="mix_ffn",
    )(*([x] * pieces), lead, o2d, wo_bf, g, win_bf, wout_bf)


def _mix_ffn_final_call(h3d, o3d, wo_bf, g, win_bf, wout_bf, final_g, skip, tm):
    b, tp, d = h3d.shape
    f = wout_bf.shape[0]
    blk = ATT_BLOCK
    pieces = tm // blk
    first = skip // blk

    def piece_spec(q):
        return pl.BlockSpec((None, blk, d), lambda bi, j: (bi, first + j * pieces + q, 0))

    in_specs = [piece_spec(q) for q in range(pieces)] * 2
    in_specs += [_const_spec((d, d)), _const_spec((1, d)), _const_spec((d, 2 * f)),
                 _const_spec((f, d)), _const_spec((1, d))]
    return pl.pallas_call(
        functools.partial(_mix_ffn_final_kernel, f=f, pieces=pieces),
        grid=(b, (tp - skip) // tm),
        in_specs=in_specs,
        out_specs=pl.BlockSpec((None, tm, d), lambda bi, j: (bi, j, 0)),
        out_shape=jax.ShapeDtypeStruct((b, tp - skip, d), F32),
        compiler_params=pltpu.CompilerParams(
            dimension_semantics=("parallel", "parallel"), vmem_limit_bytes=VMEM_LIMIT),
        name="mix_ffn_final",
    )(*([h3d] * pieces), *([o3d] * pieces), wo_bf, g, win_bf, wout_bf, final_g)


def _group_sum(x, lane_lo):
    lo = jnp.sum(jnp.where(lane_lo, x, 0.0), axis=1, keepdims=True)
    hi = jnp.sum(jnp.where(lane_lo, 0.0, x), axis=1, keepdims=True)
    return jnp.where(lane_lo, lo, hi)


def _rwkv_prep_kernel(h_ref, hprev_ref, vf_ref, g_ref, mu_ref, vec_ref, wrkv_ref,
                      w1_ref, w2_ref, a1_ref, a2_ref, v1_ref, v2_ref, g1_ref, g2_ref,
                      r_ref, k_ref, v_ref, lw_ref, kk_ref, b_ref, gate_ref, *, tm, tp):
    i = pl.program_id(0)
    g = g_ref[...]
    hn = _rms(h_ref[...], g)
    prev_last = _rms(hprev_ref[...], g)[7:8, :]
    local = lax.broadcasted_iota(jnp.int32, (tm, 1), 0)
    prev = jnp.where(local == 0, prev_last, pltpu.roll(hn, 1, axis=0))
    xx = jnp.where((i * tm + local) % tp == 0, 0.0, prev - hn)

    def mix(n):
        return (hn + xx * mu_ref[n:n + 1, :]).astype(BF16)

    w0, a0, v0 = vec_ref[0:1, :], vec_ref[1:2, :], vec_ref[2:3, :]
    k_k, k_a = vec_ref[3:4, :], vec_ref[4:5, :]

    r_ref[...] = _dg(mix(0), wrkv_ref[0])
    k = _dg(mix(1), wrkv_ref[1])
    xv = mix(2)
    v = _dg(xv, wrkv_ref[2])

    wl = w0 + _dg(jnp.tanh(_dg(mix(3), w1_ref[...])).astype(BF16), w2_ref[...])
    w_log = -(jnp.maximum(-wl, 0.0) + _softplus_neg_abs(wl)) - 0.5
    lw_ref[...] = -jnp.exp(w_log)

    v_gate = _sigmoid(v0 + _dg(_dg(xv, v1_ref[...]).astype(BF16), v2_ref[...]))
    v_ref[...] = v + (vf_ref[...] - v) * v_gate
    a = _sigmoid(a0 + _dg(_dg(mix(4), a1_ref[...]).astype(BF16), a2_ref[...]))
    gate_ref[...] = _dg(_sigmoid(_dg(mix(5), g1_ref[...])).astype(BF16), g2_ref[...])

    lane_lo = lax.broadcasted_iota(jnp.int32, (tm, LANES), 1) < HEAD_DIM
    kk = k * k_k
    for p in range(kk.shape[1] // LANES):
        sl = slice(p * LANES, (p + 1) * LANES)
        kkp = kk[:, sl]
        norm = jnp.maximum(jnp.sqrt(_group_sum(kkp * kkp, lane_lo)), L2_EPS)
        kkp = kkp / norm
        kk_ref[:, sl] = kkp
        b_ref[:, sl] = kkp * a[:, sl]
    k_ref[...] = k * (1.0 + (a - 1.0) * k_a)


def _rwkv_prep_call(h2d, vf2d, g, mu, vecs, wrkv_bf, loras, tm, tp):
    m, d = h2d.shape
    row_spec = pl.BlockSpec((tm, d), lambda i: (i, 0))
    in_specs = [row_spec,
                pl.BlockSpec((8, d), lambda i: (jnp.maximum(i * (tm // 8) - 1, 0), 0)),
                row_spec,
                _const_spec((1, d)), _const_spec(mu.shape), _const_spec(vecs.shape),
                _const_spec(wrkv_bf.shape)]
    in_specs += [_const_spec(w.shape) for w in loras]
    return pl.pallas_call(
        functools.partial(_rwkv_prep_kernel, tm=tm, tp=tp),
        grid=(m // tm,),
        in_specs=in_specs,
        out_specs=[row_spec] * 7,
        out_shape=[jax.ShapeDtypeStruct((m, d), F32)] * 7,
        compiler_params=pltpu.CompilerParams(
            dimension_semantics=("parallel",), vmem_limit_bytes=VMEM_LIMIT),
        name="rwkv_prep",
    )(h2d, h2d, vf2d, g, mu, vecs, wrkv_bf, *loras)


def _stack_heads(x, lane_lo):
    zero = jnp.zeros_like(x)
    return jnp.concatenate([jnp.where(lane_lo, x, zero), jnp.where(lane_lo, zero, x)], axis=0)


def _rwkv_chunk_setup(rs, ks, vs, lws, kks, bs, cums, passes):
    c = CHUNK
    n = 2 * c
    units = range(len(rs))
    lane_lo = lax.broadcasted_iota(jnp.int32, (c, LANES), 1) < HEAD_DIM
    st = functools.partial(_stack_heads, lane_lo=lane_lo)
    row = lax.broadcasted_iota(jnp.int32, (n, n), 0)
    col = lax.broadcasted_iota(jnp.int32, (n, n), 1)
    strict = (col % c) < (row % c)
    incl = (col % c) <= (row % c)
    in_lo = (row < c) & (col < c)
    in_hi = (row >= c) & (col >= c)
    eye = (row == col).astype(F32)

    def mm(a, b, dn=_NN):
        return _mm(a, b, dn, passes)

    def block_diag(g, tri, swap):
        g_sw = pltpu.roll(g, c, axis=1)
        lo, hi = (g_sw, g) if swap else (g, g_sw)
        return jnp.where(tri & in_lo, lo, jnp.where(tri & in_hi, hi, 0.0))

    e_inv = [jnp.exp(-cums[u]) for u in units]
    a_t = [-kks[u] * jnp.exp(cums[u] - lws[u]) for u in units]
    r_t = [rs[u] * jnp.exp(cums[u]) for u in units]
    s_v = [st(vs[u]) for u in units]
    gram = [mm(jnp.concatenate([st(a_t[u]), st(r_t[u])], axis=0),
               jnp.concatenate([bs[u] * e_inv[u], ks[u] * e_inv[u]], axis=0), _NT)
            for u in units]
    a_ab = [block_diag(gram[u][:n], strict, False) for u in units]
    ak_v = [mm(block_diag(gram[u][:n], strict, True), s_v[u]) for u in units]

    inv = [eye + a_ab[u] for u in units]
    power = a_ab
    for _ in range((c - 1).bit_length() - 1):
        power = [mm(power[u], power[u]) for u in units]
        inv = [inv[u] + mm(inv[u], power[u]) for u in units]

    totals = [cums[u][c - 1:c, :] for u in units]
    e_rem = [jnp.exp(totals[u] - cums[u]) for u in units]
    return dict(
        ar=[jnp.concatenate([a_t[u], r_t[u]], axis=0) for u in units],
        s_v=s_v, ak_v=ak_v, inv=inv,
        m_r=[jnp.concatenate([block_diag(gram[u][n:], incl, False),
                              block_diag(gram[u][n:], incl, True)], axis=1) for u in units],
        bk_rem=[jnp.concatenate([st(bs[u] * e_rem[u]), st(ks[u] * e_rem[u])], axis=0)
                for u in units],
        decay=[jnp.exp(totals[u]) for u in units])


def _rwkv_chunk_apply(setup, units, states, passes):
    c = CHUNK
    lane_lo = lax.broadcasted_iota(jnp.int32, (c, LANES), 1) < HEAD_DIM
    st = functools.partial(_stack_heads, lane_lo=lane_lo)
    idx = range(len(units))

    def mm(a, b, dn=_NN):
        return _mm(a, b, dn, passes)

    x1 = [mm(setup["ar"][units[i]], states[i], _NT) for i in idx]
    rhs = [st(x1[i][:c]) + setup["ak_v"][units[i]] for i in idx]
    s_uv = [jnp.concatenate([mm(setup["inv"][units[i]], rhs[i]), setup["s_v"][units[i]]], axis=0)
            for i in idx]
    s_y = [st(x1[i][c:]) + mm(setup["m_r"][units[i]], s_uv[i]) for i in idx]
    ys = [s_y[i][:c] + s_y[i][c:] for i in idx]
    new_states = [states[i] * setup["decay"][units[i]]
                  + mm(s_uv[i], setup["bk_rem"][units[i]], _TN) for i in idx]
    return ys, new_states


def _rwkv_scan_kernel(r_ref, k_ref, v_ref, lw_ref, kk_ref, b_ref, gate_ref,
                      rk_ref, lnw_ref, lnb_ref, out_ref, state_ref, *, passes):
    c = CHUNK
    rows = lw_ref.shape[0]
    n_chunks = rows // c
    n_groups = lw_ref.shape[1] // LANES

    @pl.when(pl.program_id(2) == 0)
    def _():
        state_ref[...] = jnp.zeros_like(state_ref)

    lw = lw_ref[...]
    rowc = lax.broadcasted_iota(jnp.int32, (rows, rows), 0)
    colc = lax.broadcasted_iota(jnp.int32, (rows, rows), 1)
    cum = _mm_exact_lhs(((colc <= rowc) & (colc // c == rowc // c)).astype(BF16), lw, n=3)

    windows = [(slice(ch * c, (ch + 1) * c), slice(p * LANES, (p + 1) * LANES))
               for ch in range(n_chunks) for p in range(n_groups)]
    units = range(len(windows))
    rs = [r_ref[w] for w in windows]
    ks = [k_ref[w] for w in windows]
    vs = [v_ref[w] for w in windows]
    setup = _rwkv_chunk_setup(rs, ks, vs, [lw[w] for w in windows], [kk_ref[w] for w in windows],
                              [b_ref[w] for w in windows], [cum[w] for w in windows], passes)
    states = [state_ref[p] for p in range(n_groups)]
    ys = []
    for ch in range(n_chunks):
        y_ch, states = _rwkv_chunk_apply(
            setup, [ch * n_groups + p for p in range(n_groups)], states, passes)
        ys += y_ch
    for p in range(n_groups):
        state_ref[p] = states[p]

    lane_lo = lax.broadcasted_iota(jnp.int32, (c, LANES), 1) < HEAD_DIM
    inv_n = 1.0 / HEAD_DIM
    means = [_group_sum(ys[u], lane_lo) * inv_n for u in units]
    devs = [ys[u] - means[u] for u in units]
    variances = [_group_sum(devs[u] * devs[u], lane_lo) * inv_n for u in units]
    bonus = [_group_sum(rs[u] * ks[u] * rk_ref[:, windows[u][1]], lane_lo) * vs[u] for u in units]
    for u in units:
        w = windows[u]
        yn = devs[u] * lax.rsqrt(variances[u] + GN_EPS) * lnw_ref[:, w[1]] + lnb_ref[:, w[1]]
        out_ref[w] = ((yn + bonus[u]) * gate_ref[w]).astype(BF16)


def _rwkv_scan_call(seqs, rk, lnw, lnb, lane_group, chunks_per_step, passes):
    b, tp, d = seqs[0].shape
    c = CHUNK * chunks_per_step
    seq_spec = pl.BlockSpec((None, c, lane_group), lambda bi, l, ci: (bi, ci, l))
    vec_spec = pl.BlockSpec((1, lane_group), lambda bi, l, ci: (0, l))
    return pl.pallas_call(
        functools.partial(_rwkv_scan_kernel, passes=passes),
        grid=(b, d // lane_group, tp // c),
        in_specs=[seq_spec] * 7 + [vec_spec] * 3,
        out_specs=seq_spec,
        out_shape=jax.ShapeDtypeStruct((b, tp, d), BF16),
        scratch_shapes=[pltpu.VMEM((lane_group // LANES, LANES, LANES), F32)],
        compiler_params=pltpu.CompilerParams(
            dimension_semantics=("parallel", "parallel", "arbitrary"),
            vmem_limit_bytes=VMEM_LIMIT),
        name="rwkv_scan",
    )(*seqs, rk, lnw, lnb)


def kernel(x, meta_tokens, attn_norm, w_qkv, w_o_attn, rwkv_norm, rwkv_mu, w_rkv, w_o_rwkv,
           w0, w1, w2, a0, a1, a2, v0, v1, v2, g1, g2, k_k, k_a, r_k, ln_x_w, ln_x_b,
           ffn_norm, w_ffn_in, w_ffn_out, final_norm):
    b, seq, d = x.shape
    n_meta = meta_tokens.shape[0]
    pad = (-n_meta) % ATT_BLOCK
    tp = pad + n_meta + seq
    assert pad + n_meta == ATT_BLOCK and seq % ATT_BLOCK == 0 and d % LANES == 0
    m = b * tp
    blocks_per_seq = tp // ATT_BLOCK
    tm = 512 if m % 512 == 0 else ATT_BLOCK
    tm_prep = 256 if m % 256 == 0 else ATT_BLOCK

    def row(vec):
        return vec.reshape(1, d).astype(F32)

    lead = jnp.concatenate([jnp.zeros((pad, d), x.dtype), meta_tokens.astype(x.dtype)], axis=0)

    qkv, v_first = _qkv_call(x, lead, row(attn_norm[0]), w_qkv[0].astype(BF16), tm,
                             blocks_per_seq)
    o = _attn_call(qkv.reshape(b, tp, 3 * d), pad, 4 * LANES).reshape(m, d)
    h = _mix_ffn_call(x, lead, o, w_o_attn[0].astype(BF16), row(ffn_norm[0]),
                      w_ffn_in[0].astype(BF16), w_ffn_out[0].astype(BF16), tm, blocks_per_seq)

    vecs = jnp.concatenate([row(w0[0]), row(a0[0]), row(v0[0]), row(k_k[0]), row(k_a[0]),
                            jnp.zeros((3, d), F32)], axis=0)
    loras = [w.astype(BF16) for w in (w1[0], w2[0], a1[0], a2[0], v1[0], v2[0], g1[0], g2[0])]
    seqs = _rwkv_prep_call(h, v_first, row(rwkv_norm[0]), rwkv_mu[0], vecs,
                           w_rkv[0].astype(BF16), loras, tm_prep, tp)
    chunks_per_step = 3 if (tp // CHUNK) % 3 == 0 else 2
    y = _rwkv_scan_call([s.reshape(b, tp, d) for s in seqs], row(r_k[0]), row(ln_x_w[0]),
                        row(ln_x_b[0]), 8 * LANES, chunks_per_step, 1)
    tm_out = 512 if seq % 512 == 0 else ATT_BLOCK
    return _mix_ffn_final_call(h.reshape(b, tp, d), y, w_o_rwkv[0].astype(BF16),
                               row(ffn_norm[1]), w_ffn_in[1].astype(BF16),
                               w_ffn_out[1].astype(BF16), row(final_norm),
                               pad + n_meta, tm_out)
```

```python
import functools

import jax
import jax.numpy as jnp
from jax import lax
from jax.experimental import pallas as pl
from jax.experimental.pallas import tpu as pltpu

F32 = jnp.float32
BF16 = jnp.bfloat16

HEAD_DIM = 64
N_META = 16
ATT_BLOCK = 128
KEY_BLOCKS_PER_TRIP = 2
LANES = 128
BF16_SUBLANES = 16
CHUNK = 64
RMS_EPS = 1e-6
GN_EPS = 64e-5
L2_EPS = 1e-12
EXP_UNDERFLOW = -88.0
VMEM_LIMIT = 56 * 1024 * 1024

_NN = (((1,), (0,)), ((), ()))
_NT = (((1,), (1,)), ((), ()))
_TN = (((0,), (0,)), ((), ()))


def _dg(a, b, dn=_NN):
    return lax.dot_general(a, b, dn, preferred_element_type=F32)


def _split(x, n):
    parts = []
    for _ in range(n):
        p = x.astype(BF16)
        parts.append(p)
        x = x - p.astype(F32)
    return parts


def _mm(a, b, dn=_NN, passes=3):
    if passes == 1:
        return _dg(a.astype(BF16), b.astype(BF16), dn)
    ah, al = _split(a, 2)
    bh, bl = _split(b, 2)
    return _dg(ah, bh, dn) + (_dg(ah, bl, dn) + _dg(al, bh, dn))


def _mm_exact_lhs(a_bf, b, dn=_NN, n=3):
    parts = _split(b, n)
    out = _dg(a_bf, parts[-1], dn)
    for p in parts[-2::-1]:
        out = out + _dg(a_bf, p, dn)
    return out


def _mm_exact_rhs(a, b_bf, dn=_NN, n=3):
    parts = _split(a, n)
    out = _dg(parts[-1], b_bf, dn)
    for p in parts[-2::-1]:
        out = out + _dg(p, b_bf, dn)
    return out


def _rms(x, g):
    return x * lax.rsqrt(jnp.mean(x * x, axis=-1, keepdims=True) + RMS_EPS) * g


def _softplus_neg_abs(z):
    return jnp.log(1.0 + jnp.exp(-jnp.abs(z)))


def _sigmoid(x):
    return 1.0 / (1.0 + jnp.exp(-x))


def _const_spec(shape):
    nd = len(shape)
    return pl.BlockSpec(shape, lambda *_: (0,) * nd, pipeline_mode=pl.Buffered(1))


def _padded_piece_specs(pieces, blocks_per_seq, d):
    def spec(q):
        def index_map(i):
            blk_id = i * pieces + q
            return blk_id // blocks_per_seq, jnp.maximum(blk_id % blocks_per_seq - 1, 0), 0
        return pl.BlockSpec((None, ATT_BLOCK, d), index_map)
    return [spec(q) for q in range(pieces)]


def _padded_tile(piece_refs, lead_ref, blocks_per_seq):
    i = pl.program_id(0)
    pieces = len(piece_refs)
    parts = [jnp.where((i * pieces + q) % blocks_per_seq == 0, lead_ref[...], r[...])
             for q, r in enumerate(piece_refs)]
    return jnp.concatenate(parts, axis=0)


def _qkv_kernel(*refs, d, pieces, blocks_per_seq):
    lead_ref, g_ref, w_ref, qkv_ref, v_ref = refs[pieces:]
    x = _padded_tile(refs[:pieces], lead_ref, blocks_per_seq)
    xb = _rms(x, g_ref[...]).astype(BF16)
    for j in range(3):
        y = _dg(xb, w_ref[:, j * d:(j + 1) * d])
        if j == 0:
            y = y * (HEAD_DIM ** -0.5)
        qkv_ref[:, j * d:(j + 1) * d] = y.astype(BF16)
        if j == 2:
            v_ref[...] = y


def _qkv_call(x, lead, g, w_bf, tm, blocks_per_seq):
    b, _, d = x.shape
    m = b * blocks_per_seq * ATT_BLOCK
    pieces = tm // ATT_BLOCK
    return pl.pallas_call(
        functools.partial(_qkv_kernel, d=d, pieces=pieces, blocks_per_seq=blocks_per_seq),
        grid=(m // tm,),
        in_specs=_padded_piece_specs(pieces, blocks_per_seq, d) + [
            _const_spec((ATT_BLOCK, d)), _const_spec((1, d)), _const_spec((d, 3 * d))],
        out_specs=[pl.BlockSpec((tm, 3 * d), lambda i: (i, 0)),
                   pl.BlockSpec((tm, d), lambda i: (i, 0))],
        out_shape=[jax.ShapeDtypeStruct((m, 3 * d), BF16),
                   jax.ShapeDtypeStruct((m, d), F32)],
        compiler_params=pltpu.CompilerParams(
            dimension_semantics=("parallel",), vmem_limit_bytes=VMEM_LIMIT),
        name="qkv",
    )(*([x] * pieces), lead, g, w_bf)


def _attn_kernel(*refs, pad, weight_tiles):
    n_weights = len(weight_tiles)
    q_ref, k_ref, v_ref = refs[:3]
    w_refs = refs[3:3 + n_weights]
    o_ref = refs[3 + n_weights]
    wbf_refs = refs[4 + n_weights:]
    step = ((pl.program_id(0) * pl.num_programs(1) + pl.program_id(1)) * pl.num_programs(2)
            + pl.program_id(2))
    for w_ref, wbf_ref, n_tiles in zip(w_refs, wbf_refs, weight_tiles):
        @pl.when(step < n_tiles)
        def _(w_ref=w_ref, wbf_ref=wbf_ref):
            wbf_ref[...] = w_ref[...].astype(BF16)

    blk = ATT_BLOCK
    n_groups = q_ref.shape[1] // LANES
    i = pl.program_id(2)
    row = lax.broadcasted_iota(jnp.int32, (blk, blk), 0)
    col = lax.broadcasted_iota(jnp.int32, (blk, blk), 1)
    lane_lo = lax.broadcasted_iota(jnp.int32, (blk, LANES), 1) < HEAD_DIM
    q_heads = []
    for p in range(n_groups):
        q = q_ref[:, p * LANES:(p + 1) * LANES]
        zero_q = jnp.zeros_like(q)
        q_heads += [jnp.where(lane_lo, q, zero_q), jnp.where(lane_lo, zero_q, q)]
    upper = (row > col).astype(BF16)

    def cond(carry):
        jj, c_max = carry[0], carry[1]
        return (jj <= i) & (c_max > EXP_UNDERFLOW)

    def body(carry):
        jj, _, accs, cs = carry
        n_heads = 2 * n_groups
        masks, kts, vts = [], [], []
        for w in range(KEY_BLOCKS_PER_TRIP):
            j = i - jj - w
            start = pl.multiple_of(jnp.maximum(j, 0) * blk, blk)
            s_idx = j * blk + col
            masks.append((s_idx < i * blk + row) & (s_idx >= pad))
            kts.append([k_ref[pl.ds(start, blk), p * LANES:(p + 1) * LANES] for p in range(n_groups)])
            vts.append([v_ref[pl.ds(start, blk), p * LANES:(p + 1) * LANES] for p in range(n_groups)])
        units = [(w, h) for w in range(KEY_BLOCKS_PER_TRIP) for h in range(n_heads)]
        zs = [_dg(q_heads[h], kts[w][h // 2], _NT) for w, h in units]
        log_beta = [jnp.minimum(z, 0.0) - _softplus_neg_abs(z) for z in zs]
        log_rest = [jnp.where(masks[w], log_beta[u] - zs[u], 0.0)
                    for u, (w, h) in enumerate(units)]
        suffix = [_mm_exact_rhs(lr, upper, n=2) for lr in log_rest]
        row_tot = [jnp.sum(lr, axis=1, keepdims=True) for lr in log_rest]
        carried = list(cs)
        outs = []
        for u, (w, h) in enumerate(units):
            att = jnp.where(masks[w], jnp.exp(log_beta[u] + suffix[u] + carried[h]), 0.0)
            outs.append(_dg(att.astype(BF16), vts[w][h // 2]))
            carried[h] = carried[h] + row_tot[u]
        new_accs = list(accs)
        for w in range(KEY_BLOCKS_PER_TRIP):
            for p in range(n_groups):
                new_accs[p] = new_accs[p] + jnp.where(lane_lo, outs[w * n_heads + 2 * p],
                                                      outs[w * n_heads + 2 * p + 1])
        c_all = carried[0]
        for c in carried[1:]:
            c_all = jnp.maximum(c_all, c)
        return jj + KEY_BLOCKS_PER_TRIP, jnp.max(c_all), tuple(new_accs), tuple(carried)

    init = (jnp.int32(0), jnp.float32(0.0),
            tuple(jnp.zeros((blk, LANES), F32) for _ in range(n_groups)),
            tuple(jnp.zeros((blk, 1), F32) for _ in range(2 * n_groups)))
    accs = lax.while_loop(cond, body, init)[2]
    for p in range(n_groups):
        o_ref[:, p * LANES:(p + 1) * LANES] = accs[p].astype(BF16)


def _cast_tile_rows(n_rows, n_steps):
    for rows in range(BF16_SUBLANES, n_rows + 1, BF16_SUBLANES):
        if n_rows % rows == 0 and n_rows // rows <= n_steps:
            return rows
    raise ValueError(f"cannot tile {n_rows} rows over {n_steps} steps")


def _attn_call(qkv3d, pad, lane_set, weights):
    b, tp, d3 = qkv3d.shape
    d = d3 // 3
    ng = d // lane_set
    blk = ATT_BLOCK
    n_blk = tp // blk
    n_steps = b * ng * n_blk
    w_in_specs, w_out_specs, w_out_shapes, weight_tiles = [], [], [], []
    for w, layer in weights:
        _, n_rows, n_cols = w.shape
        rows = _cast_tile_rows(n_rows, n_steps)
        last = n_rows // rows - 1
        weight_tiles.append(last + 1)

        def tile(bi, p, i, last=last):
            return jnp.minimum((bi * ng + p) * n_blk + i, last)

        w_in_specs.append(pl.BlockSpec(
            (None, rows, n_cols), lambda bi, p, i, layer=layer, tile=tile: (layer, tile(bi, p, i), 0)))
        w_out_specs.append(pl.BlockSpec(
            (rows, n_cols), lambda bi, p, i, tile=tile: (tile(bi, p, i), 0)))
        w_out_shapes.append(jax.ShapeDtypeStruct((n_rows, n_cols), BF16))
    outs = pl.pallas_call(
        functools.partial(_attn_kernel, pad=pad, weight_tiles=tuple(weight_tiles)),
        grid=(b, ng, n_blk),
        in_specs=[pl.BlockSpec((None, blk, lane_set), lambda bi, p, i: (bi, i, p)),
                  pl.BlockSpec((None, tp, lane_set), lambda bi, p, i: (bi, 0, ng + p)),
                  pl.BlockSpec((None, tp, lane_set), lambda bi, p, i: (bi, 0, 2 * ng + p))]
        + w_in_specs,
        out_specs=[pl.BlockSpec((None, blk, lane_set), lambda bi, p, i: (bi, i, p))] + w_out_specs,
        out_shape=[jax.ShapeDtypeStruct((b, tp, d), BF16)] + w_out_shapes,
        compiler_params=pltpu.CompilerParams(
            dimension_semantics=("arbitrary", "arbitrary", "arbitrary"),
            vmem_limit_bytes=VMEM_LIMIT),
        name="attention",
    )(qkv3d, qkv3d, qkv3d, *[w for w, _ in weights])
    return outs[0], outs[1:]


def _mix_ffn_body(h, o, wo_ref, g_ref, win_ref, wout_ref, f):
    h1 = h + _dg(o, wo_ref[...])
    xb = _rms(h1, g_ref[...]).astype(BF16)
    gu = _dg(xb, win_ref[...])
    gate = gu[:, :f]
    act = (gate * _sigmoid(gate) * gu[:, f:]).astype(BF16)
    return h1 + _dg(act, wout_ref[...])


def _mix_ffn_kernel(*refs, f, pieces, blocks_per_seq):
    lead_ref, o_ref, wo_ref, g_ref, win_ref, wout_ref, out_ref = refs[pieces:]
    h = _padded_tile(refs[:pieces], lead_ref, blocks_per_seq)
    out_ref[...] = _mix_ffn_body(h, o_ref[...], wo_ref, g_ref, win_ref, wout_ref, f)


def _mix_ffn_final_kernel(*refs, f, pieces):
    h_refs, o_refs = refs[:pieces], refs[pieces:2 * pieces]
    wo_ref, g_ref, win_ref, wout_ref, fg_ref, out_ref = refs[2 * pieces:]
    h = jnp.concatenate([r[...] for r in h_refs], axis=0)
    o = jnp.concatenate([r[...] for r in o_refs], axis=0)
    out = _mix_ffn_body(h, o, wo_ref, g_ref, win_ref, wout_ref, f)
    out_ref[...] = _rms(out, fg_ref[...])


def _mix_ffn_call(x, lead, o2d, wo_bf, g, win_bf, wout_bf, tm, blocks_per_seq):
    m, d = o2d.shape
    f = wout_bf.shape[0]
    pieces = tm // ATT_BLOCK
    return pl.pallas_call(
        functools.partial(_mix_ffn_kernel, f=f, pieces=pieces, blocks_per_seq=blocks_per_seq),
        grid=(m // tm,),
        in_specs=_padded_piece_specs(pieces, blocks_per_seq, d) + [
            _const_spec((ATT_BLOCK, d)),
            pl.BlockSpec((tm, d), lambda i: (i, 0)),
            _const_spec((d, d)),
            _const_spec((1, d)),
            _const_spec((d, 2 * f)),
            _const_spec((f, d))],
        out_specs=pl.BlockSpec((tm, d), lambda i: (i, 0)),
        out_shape=jax.ShapeDtypeStruct((m, d), F32),
        compiler_params=pltpu.CompilerParams(
            dimension_semantics=("parallel",), vmem_limit_bytes=VMEM_LIMIT),
        name="mix_ffn",
    )(*([x] * pieces), lead, o2d, wo_bf, g, win_bf, wout_bf)


def _mix_ffn_final_call(h3d, o3d, wo_bf, g, win_bf, wout_bf, final_g, skip, tm):
    b, tp, d = h3d.shape
    f = wout_bf.shape[0]
    blk = ATT_BLOCK
    pieces = tm // blk
    first = skip // blk

    def piece_spec(q):
        return pl.BlockSpec((None, blk, d), lambda bi, j: (bi, first + j * pieces + q, 0))

    in_specs = [piece_spec(q) for q in range(pieces)] * 2
    in_specs += [_const_spec((d, d)), _const_spec((1, d)), _const_spec((d, 2 * f)),
                 _const_spec((f, d)), _const_spec((1, d))]
    return pl.pallas_call(
        functools.partial(_mix_ffn_final_kernel, f=f, pieces=pieces),
        grid=(b, (tp - skip) // tm),
        in_specs=in_specs,
        out_specs=pl.BlockSpec((None, tm, d), lambda bi, j: (bi, j, 0)),
        out_shape=jax.ShapeDtypeStruct((b, tp - skip, d), F32),
        compiler_params=pltpu.CompilerParams(
            dimension_semantics=("parallel", "parallel"), vmem_limit_bytes=VMEM_LIMIT),
        name="mix_ffn_final",
    )(*([h3d] * pieces), *([o3d] * pieces), wo_bf, g, win_bf, wout_bf, final_g)


def _group_sum(x, lane_lo):
    lo = jnp.sum(jnp.where(lane_lo, x, 0.0), axis=1, keepdims=True)
    hi = jnp.sum(jnp.where(lane_lo, 0.0, x), axis=1, keepdims=True)
    return jnp.where(lane_lo, lo, hi)


def _rwkv_prep_kernel(h_ref, hprev_ref, vf_ref, g_ref, mu_ref, vec_ref, wrkv_ref,
                      w1_ref, w2_ref, a1_ref, a2_ref, v1_ref, v2_ref, g1_ref, g2_ref,
                      r_ref, k_ref, v_ref, lw_ref, kk_ref, b_ref, gate_ref, *, tm, tp):
    i = pl.program_id(0)
    g = g_ref[...]
    hn = _rms(h_ref[...], g)
    prev_last = _rms(hprev_ref[...], g)[7:8, :]
    local = lax.broadcasted_iota(jnp.int32, (tm, 1), 0)
    prev = jnp.where(local == 0, prev_last, pltpu.roll(hn, 1, axis=0))
    xx = jnp.where((i * tm + local) % tp == 0, 0.0, prev - hn)

    def mix(n):
        return (hn + xx * mu_ref[n:n + 1, :]).astype(BF16)

    w0, a0, v0 = vec_ref[0:1, :], vec_ref[1:2, :], vec_ref[2:3, :]
    k_k, k_a = vec_ref[3:4, :], vec_ref[4:5, :]

    r_ref[...] = _dg(mix(0), wrkv_ref[0])
    k = _dg(mix(1), wrkv_ref[1])
    xv = mix(2)
    v = _dg(xv, wrkv_ref[2])

    wl = w0 + _dg(jnp.tanh(_dg(mix(3), w1_ref[...])).astype(BF16), w2_ref[...])
    w_log = -(jnp.maximum(-wl, 0.0) + _softplus_neg_abs(wl)) - 0.5
    lw_ref[...] = -jnp.exp(w_log)

    v_gate = _sigmoid(v0 + _dg(_dg(xv, v1_ref[...]).astype(BF16), v2_ref[...]))
    v_ref[...] = v + (vf_ref[...] - v) * v_gate
    a = _sigmoid(a0 + _dg(_dg(mix(4), a1_ref[...]).astype(BF16), a2_ref[...]))
    gate_ref[...] = _dg(_sigmoid(_dg(mix(5), g1_ref[...])).astype(BF16), g2_ref[...])

    lane_lo = lax.broadcasted_iota(jnp.int32, (tm, LANES), 1) < HEAD_DIM
    kk = k * k_k
    for p in range(kk.shape[1] // LANES):
        sl = slice(p * LANES, (p + 1) * LANES)
        kkp = kk[:, sl]
        norm = jnp.maximum(jnp.sqrt(_group_sum(kkp * kkp, lane_lo)), L2_EPS)
        kkp = kkp / norm
        kk_ref[:, sl] = kkp
        b_ref[:, sl] = kkp * a[:, sl]
    k_ref[...] = k * (1.0 + (a - 1.0) * k_a)


def _rwkv_prep_call(h2d, vf2d, g, mu, vecs, wrkv_bf, loras, tm, tp):
    m, d = h2d.shape
    row_spec = pl.BlockSpec((tm, d), lambda i: (i, 0))
    in_specs = [row_spec,
                pl.BlockSpec((8, d), lambda i: (jnp.maximum(i * (tm // 8) - 1, 0), 0)),
                row_spec,
                _const_spec((1, d)), _const_spec(mu.shape), _const_spec(vecs.shape),
                _const_spec(wrkv_bf.shape)]
    in_specs += [_const_spec(w.shape) for w in loras]
    return pl.pallas_call(
        functools.partial(_rwkv_prep_kernel, tm=tm, tp=tp),
        grid=(m // tm,),
        in_specs=in_specs,
        out_specs=[row_spec] * 7,
        out_shape=[jax.ShapeDtypeStruct((m, d), F32)] * 7,
        compiler_params=pltpu.CompilerParams(
            dimension_semantics=("parallel",), vmem_limit_bytes=VMEM_LIMIT),
        name="rwkv_prep",
    )(h2d, h2d, vf2d, g, mu, vecs, wrkv_bf, *loras)


def _stack_heads(x, lane_lo):
    zero = jnp.zeros_like(x)
    return jnp.concatenate([jnp.where(lane_lo, x, zero), jnp.where(lane_lo, zero, x)], axis=0)


def _rwkv_chunk_setup(rs, ks, vs, lws, kks, bs, cums, passes):
    c = CHUNK
    n = 2 * c
    units = range(len(rs))
    lane_lo = lax.broadcasted_iota(jnp.int32, (c, LANES), 1) < HEAD_DIM
    st = functools.partial(_stack_heads, lane_lo=lane_lo)
    row = lax.broadcasted_iota(jnp.int32, (n, n), 0)
    col = lax.broadcasted_iota(jnp.int32, (n, n), 1)
    strict = (col % c) < (row % c)
    incl = (col % c) <= (row % c)
    in_lo = (row < c) & (col < c)
    in_hi = (row >= c) & (col >= c)
    eye = (row == col).astype(F32)

    def mm(a, b, dn=_NN):
        return _mm(a, b, dn, passes)

    def block_diag(g, tri, swap):
        g_sw = pltpu.roll(g, c, axis=1)
        lo, hi = (g_sw, g) if swap else (g, g_sw)
        return jnp.where(tri & in_lo, lo, jnp.where(tri & in_hi, hi, 0.0))

    e_inv = [jnp.exp(-cums[u]) for u in units]
    a_t = [-kks[u] * jnp.exp(cums[u] - lws[u]) for u in units]
    r_t = [rs[u] * jnp.exp(cums[u]) for u in units]
    s_v = [st(vs[u]) for u in units]
    gram = [mm(jnp.concatenate([st(a_t[u]), st(r_t[u])], axis=0),
               jnp.concatenate([bs[u] * e_inv[u], ks[u] * e_inv[u]], axis=0), _NT)
            for u in units]
    a_ab = [block_diag(gram[u][:n], strict, False) for u in units]
    ak_v = [mm(block_diag(gram[u][:n], strict, True), s_v[u]) for u in units]

    inv = [eye + a_ab[u] for u in units]
    power = a_ab
    for _ in range((c - 1).bit_length() - 1):
        power = [mm(power[u], power[u]) for u in units]
        inv = [inv[u] + mm(inv[u], power[u]) for u in units]

    totals = [cums[u][c - 1:c, :] for u in units]
    e_rem = [jnp.exp(totals[u] - cums[u]) for u in units]
    return dict(
        ar=[jnp.concatenate([a_t[u], r_t[u]], axis=0) for u in units],
        s_v=s_v, ak_v=ak_v, inv=inv,
        m_r=[jnp.concatenate([block_diag(gram[u][n:], incl, False),
                              block_diag(gram[u][n:], incl, True)], axis=1) for u in units],
        bk_rem=[jnp.concatenate([st(bs[u] * e_rem[u]), st(ks[u] * e_rem[u])], axis=0)
                for u in units],
        decay=[jnp.exp(totals[u]) for u in units])


def _rwkv_chunk_apply(setup, units, states, passes):
    c = CHUNK
    lane_lo = lax.broadcasted_iota(jnp.int32, (c, LANES), 1) < HEAD_DIM
    st = functools.partial(_stack_heads, lane_lo=lane_lo)
    idx = range(len(units))

    def mm(a, b, dn=_NN):
        return _mm(a, b, dn, passes)

    x1 = [mm(setup["ar"][units[i]], states[i], _NT) for i in idx]
    rhs = [st(x1[i][:c]) + setup["ak_v"][units[i]] for i in idx]
    s_uv = [jnp.concatenate([mm(setup["inv"][units[i]], rhs[i]), setup["s_v"][units[i]]], axis=0)
            for i in idx]
    s_y = [st(x1[i][c:]) + mm(setup["m_r"][units[i]], s_uv[i]) for i in idx]
    ys = [s_y[i][:c] + s_y[i][c:] for i in idx]
    new_states = [states[i] * setup["decay"][units[i]]
                  + mm(s_uv[i], setup["bk_rem"][units[i]], _TN) for i in idx]
    return ys, new_states


def _rwkv_scan_kernel(r_ref, k_ref, v_ref, lw_ref, kk_ref, b_ref, gate_ref,
                      rk_ref, lnw_ref, lnb_ref, out_ref, state_ref, *, passes):
    c = CHUNK
    rows = lw_ref.shape[0]
    n_chunks = rows // c
    n_groups = lw_ref.shape[1] // LANES

    @pl.when(pl.program_id(2) == 0)
    def _():
        state_ref[...] = jnp.zeros_like(state_ref)

    lw = lw_ref[...]
    rowc = lax.broadcasted_iota(jnp.int32, (rows, rows), 0)
    colc = lax.broadcasted_iota(jnp.int32, (rows, rows), 1)
    cum = _mm_exact_lhs(((colc <= rowc) & (colc // c == rowc // c)).astype(BF16), lw, n=3)

    windows = [(slice(ch * c, (ch + 1) * c), slice(p * LANES, (p + 1) * LANES))
               for ch in range(n_chunks) for p in range(n_groups)]
    units = range(len(windows))
    rs = [r_ref[w] for w in windows]
    ks = [k_ref[w] for w in windows]
    vs = [v_ref[w] for w in windows]
    setup = _rwkv_chunk_setup(rs, ks, vs, [lw[w] for w in windows], [kk_ref[w] for w in windows],
                              [b_ref[w] for w in windows], [cum[w] for w in windows], passes)
    states = [state_ref[p] for p in range(n_groups)]
    ys = []
    for ch in range(n_chunks):
        y_ch, states = _rwkv_chunk_apply(
            setup, [ch * n_groups + p for p in range(n_groups)], states, passes)
        ys += y_ch
    for p in range(n_groups):
        state_ref[p] = states[p]

    lane_lo = lax.broadcasted_iota(jnp.int32, (c, LANES), 1) < HEAD_DIM
    inv_n = 1.0 / HEAD_DIM
    means = [_group_sum(ys[u], lane_lo) * inv_n for u in units]
    devs = [ys[u] - means[u] for u in units]
    variances = [_group_sum(devs[u] * devs[u], lane_lo) * inv_n for u in units]
    bonus = [_group_sum(rs[u] * ks[u] * rk_ref[:, windows[u][1]], lane_lo) * vs[u] for u in units]
    for u in units:
        w = windows[u]
        yn = devs[u] * lax.rsqrt(variances[u] + GN_EPS) * lnw_ref[:, w[1]] + lnb_ref[:, w[1]]
        out_ref[w] = ((yn + bonus[u]) * gate_ref[w]).astype(BF16)


def _rwkv_scan_call(seqs, rk, lnw, lnb, lane_group, chunks_per_step, passes):
    b, tp, d = seqs[0].shape
    c = CHUNK * chunks_per_step
    seq_spec = pl.BlockSpec((None, c, lane_group), lambda bi, l, ci: (bi, ci, l))
    vec_spec = pl.BlockSpec((1, lane_group), lambda bi, l, ci: (0, l))
    return pl.pallas_call(
        functools.partial(_rwkv_scan_kernel, passes=passes),
        grid=(b, d // lane_group, tp // c),
        in_specs=[seq_spec] * 7 + [vec_spec] * 3,
        out_specs=seq_spec,
        out_shape=jax.ShapeDtypeStruct((b, tp, d), BF16),
        scratch_shapes=[pltpu.VMEM((lane_group // LANES, LANES, LANES), F32)],
        compiler_params=pltpu.CompilerParams(
            dimension_semantics=("parallel", "parallel", "arbitrary"),
            vmem_limit_bytes=VMEM_LIMIT),
        name="rwkv_scan",
    )(*seqs, rk, lnw, lnb)


def kernel(x, meta_tokens, attn_norm, w_qkv, w_o_attn, rwkv_norm, rwkv_mu, w_rkv, w_o_rwkv,
           w0, w1, w2, a0, a1, a2, v0, v1, v2, g1, g2, k_k, k_a, r_k, ln_x_w, ln_x_b,
           ffn_norm, w_ffn_in, w_ffn_out, final_norm):
    b, seq, d = x.shape
    n_meta = meta_tokens.shape[0]
    pad = (-n_meta) % ATT_BLOCK
    tp = pad + n_meta + seq
    assert pad + n_meta == ATT_BLOCK and seq % ATT_BLOCK == 0 and d % LANES == 0
    m = b * tp
    blocks_per_seq = tp // ATT_BLOCK
    tm = 512 if m % 512 == 0 else ATT_BLOCK
    tm_prep = 256 if m % 256 == 0 else ATT_BLOCK

    def row(vec):
        return vec.reshape(1, d).astype(F32)

    lead = jnp.concatenate([jnp.zeros((pad, d), x.dtype), meta_tokens.astype(x.dtype)], axis=0)

    qkv, v_first = _qkv_call(x, lead, row(attn_norm[0]), w_qkv[0].astype(BF16), tm,
                             blocks_per_seq)
    o, (wo0, win0, wout0, wrkv, wo1, win1, wout1) = _attn_call(
        qkv.reshape(b, tp, 3 * d), pad, 4 * LANES,
        [(w_o_attn, 0), (w_ffn_in, 0), (w_ffn_out, 0), (w_rkv[0].reshape(1, 3 * d, d), 0),
         (w_o_rwkv, 0), (w_ffn_in, 1), (w_ffn_out, 1)])
    h = _mix_ffn_call(x, lead, o.reshape(m, d), wo0, row(ffn_norm[0]), win0, wout0, tm,
                      blocks_per_seq)

    vecs = jnp.concatenate([row(w0[0]), row(a0[0]), row(v0[0]), row(k_k[0]), row(k_a[0]),
                            jnp.zeros((3, d), F32)], axis=0)
    loras = [w.astype(BF16) for w in (w1[0], w2[0], a1[0], a2[0], v1[0], v2[0], g1[0], g2[0])]
    seqs = _rwkv_prep_call(h, v_first, row(rwkv_norm[0]), rwkv_mu[0], vecs,
                           wrkv.reshape(3, d, d), loras, tm_prep, tp)
    chunks_per_step = 3 if (tp // CHUNK) % 3 == 0 else 2
    y = _rwkv_scan_call([s.reshape(b, tp, d) for s in seqs], row(r_k[0]), row(ln_x_w[0]),
                        row(ln_x_b[0]), 8 * LANES, chunks_per_step, 1)
    tm_out = 512 if seq % 512 == 0 else ATT_BLOCK
    return _mix_ffn_final_call(h.reshape(b, tp, d), y, wo1, row(ffn_norm[1]), win1, wout1,
                               row(final_norm), pad + n_meta, tm_out)
```

```python
import functools

import jax
import jax.numpy as jnp
from jax import lax
from jax.experimental import pallas as pl
from jax.experimental.pallas import tpu as pltpu

F32 = jnp.float32
BF16 = jnp.bfloat16

HEAD_DIM = 64
N_META = 16
ATT_BLOCK = 128
KEY_BLOCKS_PER_TRIP = 2
LANES = 128
BF16_SUBLANES = 16
CHUNK = 64
RMS_EPS = 1e-6
GN_EPS = 64e-5
L2_EPS = 1e-12
EXP2_UNDERFLOW = -127.0
LOG2_E = 1.4426950408889634
VMEM_LIMIT = 56 * 1024 * 1024

_NN = (((1,), (0,)), ((), ()))
_NT = (((1,), (1,)), ((), ()))
_TN = (((0,), (0,)), ((), ()))


def _dg(a, b, dn=_NN):
    return lax.dot_general(a, b, dn, preferred_element_type=F32)


def _split(x, n):
    parts = []
    for _ in range(n):
        p = x.astype(BF16)
        parts.append(p)
        x = x - p.astype(F32)
    return parts


def _mm(a, b, dn=_NN, passes=3):
    if passes == 1:
        return _dg(a.astype(BF16), b.astype(BF16), dn)
    ah, al = _split(a, 2)
    bh, bl = _split(b, 2)
    return _dg(ah, bh, dn) + (_dg(ah, bl, dn) + _dg(al, bh, dn))


def _mm_exact_lhs(a_bf, b, dn=_NN, n=3):
    parts = _split(b, n)
    out = _dg(a_bf, parts[-1], dn)
    for p in parts[-2::-1]:
        out = out + _dg(a_bf, p, dn)
    return out


def _mm_exact_rhs(a, b_bf, dn=_NN, n=3):
    parts = _split(a, n)
    out = _dg(parts[-1], b_bf, dn)
    for p in parts[-2::-1]:
        out = out + _dg(p, b_bf, dn)
    return out


def _rms(x, g):
    return x * lax.rsqrt(jnp.mean(x * x, axis=-1, keepdims=True) + RMS_EPS) * g


def _softplus_neg_abs(z):
    return jnp.log(1.0 + jnp.exp(-jnp.abs(z)))


def _sigmoid(x):
    return 1.0 / (1.0 + jnp.exp(-x))


def _const_spec(shape):
    nd = len(shape)
    return pl.BlockSpec(shape, lambda *_: (0,) * nd, pipeline_mode=pl.Buffered(1))


def _cast_tile_rows(n_rows, n_steps):
    for rows in range(BF16_SUBLANES, n_rows + 1, BF16_SUBLANES):
        if n_rows % rows == 0 and n_rows // rows <= n_steps:
            return rows
    raise ValueError(f"cannot tile {n_rows} rows over {n_steps} steps")


def _cast_specs(weights, n_steps):
    in_specs, out_specs, out_shapes, n_tiles = [], [], [], []
    for w, layer in weights:
        _, n_rows, n_cols = w.shape
        rows = _cast_tile_rows(n_rows, n_steps)
        last = n_rows // rows - 1
        n_tiles.append(last + 1)
        in_specs.append(pl.BlockSpec(
            (None, rows, n_cols), lambda i, layer=layer, last=last: (layer, jnp.minimum(i, last), 0)))
        out_specs.append(pl.BlockSpec(
            (rows, n_cols), lambda i, last=last: (jnp.minimum(i, last), 0)))
        out_shapes.append(jax.ShapeDtypeStruct((n_rows, n_cols), BF16))
    return in_specs, out_specs, out_shapes, tuple(n_tiles)


def _cast_weight_tiles(w_refs, wbf_refs, n_tiles):
    step = pl.program_id(0)
    for w_ref, wbf_ref, n in zip(w_refs, wbf_refs, n_tiles):
        @pl.when(step < n)
        def _(w_ref=w_ref, wbf_ref=wbf_ref):
            wbf_ref[...] = w_ref[...].astype(BF16)


def _padded_piece_specs(pieces, blocks_per_seq, d):
    def spec(q):
        def index_map(i):
            blk_id = i * pieces + q
            return blk_id // blocks_per_seq, jnp.maximum(blk_id % blocks_per_seq - 1, 0), 0
        return pl.BlockSpec((None, ATT_BLOCK, d), index_map)
    return [spec(q) for q in range(pieces)]


def _padded_tile(piece_refs, lead_ref, blocks_per_seq):
    i = pl.program_id(0)
    pieces = len(piece_refs)
    parts = [jnp.where((i * pieces + q) % blocks_per_seq == 0, lead_ref[...], r[...])
             for q, r in enumerate(piece_refs)]
    return jnp.concatenate(parts, axis=0)


def _qkv_kernel(*refs, d, pieces, blocks_per_seq, cast_tiles):
    n_cast = len(cast_tiles)
    lead_ref, g_ref, w_ref = refs[pieces:pieces + 3]
    cast_in = refs[pieces + 3:pieces + 3 + n_cast]
    qkv_ref, v_ref = refs[pieces + 3 + n_cast:pieces + 5 + n_cast]
    _cast_weight_tiles(cast_in, refs[pieces + 5 + n_cast:], cast_tiles)
    x = _padded_tile(refs[:pieces], lead_ref, blocks_per_seq)
    xb = _rms(x, g_ref[...]).astype(BF16)
    for j in range(3):
        y = _dg(xb, w_ref[:, j * d:(j + 1) * d])
        if j == 0:
            y = y * (HEAD_DIM ** -0.5)
        qkv_ref[:, j * d:(j + 1) * d] = y.astype(BF16)
        if j == 2:
            v_ref[...] = y


def _qkv_call(x, lead, g, w_bf, tm, blocks_per_seq, cast_weights):
    b, _, d = x.shape
    m = b * blocks_per_seq * ATT_BLOCK
    pieces = tm // ATT_BLOCK
    c_in, c_out, c_shapes, c_tiles = _cast_specs(cast_weights, m // tm)
    outs = pl.pallas_call(
        functools.partial(_qkv_kernel, d=d, pieces=pieces, blocks_per_seq=blocks_per_seq,
                          cast_tiles=c_tiles),
        grid=(m // tm,),
        in_specs=_padded_piece_specs(pieces, blocks_per_seq, d) + [
            _const_spec((ATT_BLOCK, d)), _const_spec((1, d)), _const_spec((d, 3 * d))] + c_in,
        out_specs=[pl.BlockSpec((tm, 3 * d), lambda i: (i, 0)),
                   pl.BlockSpec((tm, d), lambda i: (i, 0))] + c_out,
        out_shape=[jax.ShapeDtypeStruct((m, 3 * d), BF16),
                   jax.ShapeDtypeStruct((m, d), F32)] + c_shapes,
        compiler_params=pltpu.CompilerParams(
            dimension_semantics=("arbitrary",), vmem_limit_bytes=VMEM_LIMIT),
        name="qkv",
    )(*([x] * pieces), lead, g, w_bf, *[w for w, _ in cast_weights])
    return outs[0], outs[1], outs[2:]


def _attn_kernel(q_ref, k_ref, v_ref, o_ref, *, pad):
    blk = ATT_BLOCK
    n_groups = q_ref.shape[1] // LANES
    i = pl.program_id(2)
    row = lax.broadcasted_iota(jnp.int32, (blk, blk), 0)
    col = lax.broadcasted_iota(jnp.int32, (blk, blk), 1)
    lane_lo = lax.broadcasted_iota(jnp.int32, (blk, LANES), 1) < HEAD_DIM
    q_heads = []
    for p in range(n_groups):
        q = q_ref[:, p * LANES:(p + 1) * LANES]
        zero_q = jnp.zeros_like(q)
        q_heads += [jnp.where(lane_lo, q, zero_q), jnp.where(lane_lo, zero_q, q)]
    upper = (row > col).astype(BF16)

    def cond(carry):
        jj, c_max = carry[0], carry[1]
        return (jj <= i) & (c_max > EXP2_UNDERFLOW)

    def body(carry):
        jj, _, accs, cs = carry
        n_heads = 2 * n_groups
        masks, kts, vts = [], [], []
        for w in range(KEY_BLOCKS_PER_TRIP):
            j = i - jj - w
            start = pl.multiple_of(jnp.maximum(j, 0) * blk, blk)
            s_idx = j * blk + col
            masks.append((s_idx < i * blk + row) & (s_idx >= pad))
            kts.append([k_ref[pl.ds(start, blk), p * LANES:(p + 1) * LANES] for p in range(n_groups)])
            vts.append([v_ref[pl.ds(start, blk), p * LANES:(p + 1) * LANES] for p in range(n_groups)])
        units = [(w, h) for w in range(KEY_BLOCKS_PER_TRIP) for h in range(n_heads)]
        zs = [_dg(q_heads[h], kts[w][h // 2], _NT) * LOG2_E for w, h in units]
        log_beta = [jnp.minimum(z, 0.0) - jnp.log2(1.0 + jnp.exp2(-jnp.abs(z))) for z in zs]
        log_rest = [jnp.where(masks[w], log_beta[u] - zs[u], 0.0)
                    for u, (w, h) in enumerate(units)]
        suffix = [_mm_exact_rhs(lr, upper, n=2) for lr in log_rest]
        row_tot = [jnp.sum(lr, axis=1, keepdims=True) for lr in log_rest]
        carried = list(cs)
        outs = []
        for u, (w, h) in enumerate(units):
            att = jnp.where(masks[w], jnp.exp2(log_beta[u] + suffix[u] + carried[h]), 0.0)
            outs.append(_dg(att.astype(BF16), vts[w][h // 2]))
            carried[h] = carried[h] + row_tot[u]
        new_accs = list(accs)
        for w in range(KEY_BLOCKS_PER_TRIP):
            for p in range(n_groups):
                new_accs[p] = new_accs[p] + jnp.where(lane_lo, outs[w * n_heads + 2 * p],
                                                      outs[w * n_heads + 2 * p + 1])
        c_all = carried[0]
        for c in carried[1:]:
            c_all = jnp.maximum(c_all, c)
        return jj + KEY_BLOCKS_PER_TRIP, jnp.max(c_all), tuple(new_accs), tuple(carried)

    init = (jnp.int32(0), jnp.float32(0.0),
            tuple(jnp.zeros((blk, LANES), F32) for _ in range(n_groups)),
            tuple(jnp.zeros((blk, 1), F32) for _ in range(2 * n_groups)))
    accs = lax.while_loop(cond, body, init)[2]
    for p in range(n_groups):
        o_ref[:, p * LANES:(p + 1) * LANES] = accs[p].astype(BF16)


def _attn_call(qkv3d, pad, lane_set):
    b, tp, d3 = qkv3d.shape
    d = d3 // 3
    ng = d // lane_set
    blk = ATT_BLOCK
    return pl.pallas_call(
        functools.partial(_attn_kernel, pad=pad),
        grid=(b, ng, tp // blk),
        in_specs=[pl.BlockSpec((None, blk, lane_set), lambda bi, p, i: (bi, i, p)),
                  pl.BlockSpec((None, tp, lane_set), lambda bi, p, i: (bi, 0, ng + p)),
                  pl.BlockSpec((None, tp, lane_set), lambda bi, p, i: (bi, 0, 2 * ng + p))],
        out_specs=pl.BlockSpec((None, blk, lane_set), lambda bi, p, i: (bi, i, p)),
        out_shape=jax.ShapeDtypeStruct((b, tp, d), BF16),
        compiler_params=pltpu.CompilerParams(
            dimension_semantics=("parallel", "parallel", "arbitrary"),
            vmem_limit_bytes=VMEM_LIMIT),
        name="attention",
    )(qkv3d, qkv3d, qkv3d)


def _mix_ffn_body(h, o, wo_ref, g_ref, win_ref, wout_ref, f):
    h1 = h + _dg(o, wo_ref[...])
    xb = _rms(h1, g_ref[...]).astype(BF16)
    gu = _dg(xb, win_ref[...])
    gate = gu[:, :f]
    act = (gate * _sigmoid(gate) * gu[:, f:]).astype(BF16)
    return h1 + _dg(act, wout_ref[...])


def _mix_ffn_kernel(*refs, f, pieces, blocks_per_seq, cast_tiles):
    n_cast = len(cast_tiles)
    lead_ref, o_ref, wo_ref, g_ref, win_ref, wout_ref = refs[pieces:pieces + 6]
    cast_in = refs[pieces + 6:pieces + 6 + n_cast]
    out_ref = refs[pieces + 6 + n_cast]
    _cast_weight_tiles(cast_in, refs[pieces + 7 + n_cast:], cast_tiles)
    h = _padded_tile(refs[:pieces], lead_ref, blocks_per_seq)
    out_ref[...] = _mix_ffn_body(h, o_ref[...], wo_ref, g_ref, win_ref, wout_ref, f)


def _mix_ffn_final_kernel(*refs, f, pieces):
    h_refs, o_refs = refs[:pieces], refs[pieces:2 * pieces]
    wo_ref, g_ref, win_ref, wout_ref, fg_ref, out_ref = refs[2 * pieces:]
    h = jnp.concatenate([r[...] for r in h_refs], axis=0)
    o = jnp.concatenate([r[...] for r in o_refs], axis=0)
    out = _mix_ffn_body(h, o, wo_ref, g_ref, win_ref, wout_ref, f)
    out_ref[...] = _rms(out, fg_ref[...])


def _mix_ffn_call(x, lead, o2d, wo_bf, g, win_bf, wout_bf, tm, blocks_per_seq, cast_weights):
    m, d = o2d.shape
    f = wout_bf.shape[0]
    pieces = tm // ATT_BLOCK
    c_in, c_out, c_shapes, c_tiles = _cast_specs(cast_weights, m // tm)
    outs = pl.pallas_call(
        functools.partial(_mix_ffn_kernel, f=f, pieces=pieces, blocks_per_seq=blocks_per_seq,
                          cast_tiles=c_tiles),
        grid=(m // tm,),
        in_specs=_padded_piece_specs(pieces, blocks_per_seq, d) + [
            _const_spec((ATT_BLOCK, d)),
            pl.BlockSpec((tm, d), lambda i: (i, 0)),
            _const_spec((d, d)),
            _const_spec((1, d)),
            _const_spec((d, 2 * f)),
            _const_spec((f, d))] + c_in,
        out_specs=[pl.BlockSpec((tm, d), lambda i: (i, 0))] + c_out,
        out_shape=[jax.ShapeDtypeStruct((m, d), F32)] + c_shapes,
        compiler_params=pltpu.CompilerParams(
            dimension_semantics=("arbitrary",), vmem_limit_bytes=VMEM_LIMIT),
        name="mix_ffn",
    )(*([x] * pieces), lead, o2d, wo_bf, g, win_bf, wout_bf, *[w for w, _ in cast_weights])
    return outs[0], outs[1:]


def _mix_ffn_final_call(h3d, o3d, wo_bf, g, win_bf, wout_bf, final_g, skip, tm):
    b, tp, d = h3d.shape
    f = wout_bf.shape[0]
    blk = ATT_BLOCK
    pieces = tm // blk
    first = skip // blk

    def piece_spec(q):
        return pl.BlockSpec((None, blk, d), lambda bi, j: (bi, first + j * pieces + q, 0))

    in_specs = [piece_spec(q) for q in range(pieces)] * 2
    in_specs += [_const_spec((d, d)), _const_spec((1, d)), _const_spec((d, 2 * f)),
                 _const_spec((f, d)), _const_spec((1, d))]
    return pl.pallas_call(
        functools.partial(_mix_ffn_final_kernel, f=f, pieces=pieces),
        grid=(b, (tp - skip) // tm),
        in_specs=in_specs,
        out_specs=pl.BlockSpec((None, tm, d), lambda bi, j: (bi, j, 0)),
        out_shape=jax.ShapeDtypeStruct((b, tp - skip, d), F32),
        compiler_params=pltpu.CompilerParams(
            dimension_semantics=("parallel", "parallel"), vmem_limit_bytes=VMEM_LIMIT),
        name="mix_ffn_final",
    )(*([h3d] * pieces), *([o3d] * pieces), wo_bf, g, win_bf, wout_bf, final_g)


def _group_sum(x, lane_lo):
    lo = jnp.sum(jnp.where(lane_lo, x, 0.0), axis=1, keepdims=True)
    hi = jnp.sum(jnp.where(lane_lo, 0.0, x), axis=1, keepdims=True)
    return jnp.where(lane_lo, lo, hi)


def _rwkv_prep_kernel(h_ref, hprev_ref, vf_ref, g_ref, mu_ref, vec_ref, wrkv_ref,
                      w1_ref, w2_ref, a1_ref, a2_ref, v1_ref, v2_ref, g1_ref, g2_ref,
                      r_ref, k_ref, v_ref, lw_ref, kk_ref, b_ref, gate_ref, *, tm, tp):
    i = pl.program_id(0)
    g = g_ref[...]
    hn = _rms(h_ref[...], g)
    prev_last = _rms(hprev_ref[...], g)[7:8, :]
    local = lax.broadcasted_iota(jnp.int32, (tm, 1), 0)
    prev = jnp.where(local == 0, prev_last, pltpu.roll(hn, 1, axis=0))
    xx = jnp.where((i * tm + local) % tp == 0, 0.0, prev - hn)

    def mix(n):
        return (hn + xx * mu_ref[n:n + 1, :]).astype(BF16)

    w0, a0, v0 = vec_ref[0:1, :], vec_ref[1:2, :], vec_ref[2:3, :]
    k_k, k_a = vec_ref[3:4, :], vec_ref[4:5, :]

    r_ref[...] = _dg(mix(0), wrkv_ref[0])
    k = _dg(mix(1), wrkv_ref[1])
    xv = mix(2)
    v = _dg(xv, wrkv_ref[2])

    wl = w0 + _dg(jnp.tanh(_dg(mix(3), w1_ref[...])).astype(BF16), w2_ref[...])
    w_log = -(jnp.maximum(-wl, 0.0) + _softplus_neg_abs(wl)) - 0.5
    lw_ref[...] = -jnp.exp(w_log)

    v_gate = _sigmoid(v0 + _dg(_dg(xv, v1_ref[...]).astype(BF16), v2_ref[...]))
    v_ref[...] = v + (vf_ref[...] - v) * v_gate
    a = _sigmoid(a0 + _dg(_dg(mix(4), a1_ref[...]).astype(BF16), a2_ref[...]))
    gate_ref[...] = _dg(_sigmoid(_dg(mix(5), g1_ref[...])).astype(BF16), g2_ref[...])

    lane_lo = lax.broadcasted_iota(jnp.int32, (tm, LANES), 1) < HEAD_DIM
    kk = k * k_k
    for p in range(kk.shape[1] // LANES):
        sl = slice(p * LANES, (p + 1) * LANES)
        kkp = kk[:, sl]
        norm = jnp.maximum(jnp.sqrt(_group_sum(kkp * kkp, lane_lo)), L2_EPS)
        kkp = kkp / norm
        kk_ref[:, sl] = kkp
        b_ref[:, sl] = kkp * a[:, sl]
    k_ref[...] = k * (1.0 + (a - 1.0) * k_a)


def _rwkv_prep_call(h2d, vf2d, g, mu, vecs, wrkv_bf, loras, tm, tp):
    m, d = h2d.shape
    row_spec = pl.BlockSpec((tm, d), lambda i: (i, 0))
    in_specs = [row_spec,
                pl.BlockSpec((8, d), lambda i: (jnp.maximum(i * (tm // 8) - 1, 0), 0)),
                row_spec,
                _const_spec((1, d)), _const_spec(mu.shape), _const_spec(vecs.shape),
                _const_spec(wrkv_bf.shape)]
    in_specs += [_const_spec(w.shape) for w in loras]
    return pl.pallas_call(
        functools.partial(_rwkv_prep_kernel, tm=tm, tp=tp),
        grid=(m // tm,),
        in_specs=in_specs,
        out_specs=[row_spec] * 7,
        out_shape=[jax.ShapeDtypeStruct((m, d), F32)] * 7,
        compiler_params=pltpu.CompilerParams(
            dimension_semantics=("parallel",), vmem_limit_bytes=VMEM_LIMIT),
        name="rwkv_prep",
    )(h2d, h2d, vf2d, g, mu, vecs, wrkv_bf, *loras)


def _stack_heads(x, lane_lo):
    zero = jnp.zeros_like(x)
    return jnp.concatenate([jnp.where(lane_lo, x, zero), jnp.where(lane_lo, zero, x)], axis=0)


def _rwkv_chunk_setup(rs, ks, vs, lws, kks, bs, cums, passes):
    c = CHUNK
    n = 2 * c
    units = range(len(rs))
    lane_lo = lax.broadcasted_iota(jnp.int32, (c, LANES), 1) < HEAD_DIM
    st = functools.partial(_stack_heads, lane_lo=lane_lo)
    row = lax.broadcasted_iota(jnp.int32, (n, n), 0)
    col = lax.broadcasted_iota(jnp.int32, (n, n), 1)
    strict = (col % c) < (row % c)
    incl = (col % c) <= (row % c)
    in_lo = (row < c) & (col < c)
    in_hi = (row >= c) & (col >= c)
    eye = (row == col).astype(F32)

    def mm(a, b, dn=_NN):
        return _mm(a, b, dn, passes)

    def block_diag(g, tri, swap):
        g_sw = pltpu.roll(g, c, axis=1)
        lo, hi = (g_sw, g) if swap else (g, g_sw)
        return jnp.where(tri & in_lo, lo, jnp.where(tri & in_hi, hi, 0.0))

    e_inv = [jnp.exp(-cums[u]) for u in units]
    a_t = [-kks[u] * jnp.exp(cums[u] - lws[u]) for u in units]
    r_t = [rs[u] * jnp.exp(cums[u]) for u in units]
    s_v = [st(vs[u]) for u in units]
    gram = [mm(jnp.concatenate([st(a_t[u]), st(r_t[u])], axis=0),
               jnp.concatenate([bs[u] * e_inv[u], ks[u] * e_inv[u]], axis=0), _NT)
            for u in units]
    a_ab = [block_diag(gram[u][:n], strict, False) for u in units]
    ak_v = [mm(block_diag(gram[u][:n], strict, True), s_v[u]) for u in units]

    inv = [eye + a_ab[u] for u in units]
    power = a_ab
    for _ in range((c - 1).bit_length() - 1):
        power = [mm(power[u], power[u]) for u in units]
        inv = [inv[u] + mm(inv[u], power[u]) for u in units]

    totals = [cums[u][c - 1:c, :] for u in units]
    e_rem = [jnp.exp(totals[u] - cums[u]) for u in units]
    return dict(
        ar=[jnp.concatenate([a_t[u], r_t[u]], axis=0) for u in units],
        s_v=s_v, ak_v=ak_v, inv=inv,
        m_r=[jnp.concatenate([block_diag(gram[u][n:], incl, False),
                              block_diag(gram[u][n:], incl, True)], axis=1) for u in units],
        bk_rem=[jnp.concatenate([st(bs[u] * e_rem[u]), st(ks[u] * e_rem[u])], axis=0)
                for u in units],
        decay=[jnp.exp(totals[u]) for u in units])


def _rwkv_chunk_apply(setup, units, states, passes):
    c = CHUNK
    lane_lo = lax.broadcasted_iota(jnp.int32, (c, LANES), 1) < HEAD_DIM
    st = functools.partial(_stack_heads, lane_lo=lane_lo)
    idx = range(len(units))

    def mm(a, b, dn=_NN):
        return _mm(a, b, dn, passes)

    x1 = [mm(setup["ar"][units[i]], states[i], _NT) for i in idx]
    rhs = [st(x1[i][:c]) + setup["ak_v"][units[i]] for i in idx]
    s_uv = [jnp.concatenate([mm(setup["inv"][units[i]], rhs[i]), setup["s_v"][units[i]]], axis=0)
            for i in idx]
    s_y = [st(x1[i][c:]) + mm(setup["m_r"][units[i]], s_uv[i]) for i in idx]
    ys = [s_y[i][:c] + s_y[i][c:] for i in idx]
    new_states = [states[i] * setup["decay"][units[i]]
                  + mm(s_uv[i], setup["bk_rem"][units[i]], _TN) for i in idx]
    return ys, new_states


def _rwkv_scan_kernel(r_ref, k_ref, v_ref, lw_ref, kk_ref, b_ref, gate_ref,
                      rk_ref, lnw_ref, lnb_ref, out_ref, state_ref, *, passes):
    c = CHUNK
    rows = lw_ref.shape[0]
    n_chunks = rows // c
    n_groups = lw_ref.shape[1] // LANES

    @pl.when(pl.program_id(2) == 0)
    def _():
        state_ref[...] = jnp.zeros_like(state_ref)

    lw = lw_ref[...]
    rowc = lax.broadcasted_iota(jnp.int32, (rows, rows), 0)
    colc = lax.broadcasted_iota(jnp.int32, (rows, rows), 1)
    cum = _mm_exact_lhs(((colc <= rowc) & (colc // c == rowc // c)).astype(BF16), lw, n=3)

    windows = [(slice(ch * c, (ch + 1) * c), slice(p * LANES, (p + 1) * LANES))
               for ch in range(n_chunks) for p in range(n_groups)]
    units = range(len(windows))
    rs = [r_ref[w] for w in windows]
    ks = [k_ref[w] for w in windows]
    vs = [v_ref[w] for w in windows]
    setup = _rwkv_chunk_setup(rs, ks, vs, [lw[w] for w in windows], [kk_ref[w] for w in windows],
                              [b_ref[w] for w in windows], [cum[w] for w in windows], passes)
    states = [state_ref[p] for p in range(n_groups)]
    ys = []
    for ch in range(n_chunks):
        y_ch, states = _rwkv_chunk_apply(
            setup, [ch * n_groups + p for p in range(n_groups)], states, passes)
        ys += y_ch
    for p in range(n_groups):
        state_ref[p] = states[p]

    lane_lo = lax.broadcasted_iota(jnp.int32, (c, LANES), 1) < HEAD_DIM
    inv_n = 1.0 / HEAD_DIM
    means = [_group_sum(ys[u], lane_lo) * inv_n for u in units]
    devs = [ys[u] - means[u] for u in units]
    variances = [_group_sum(devs[u] * devs[u], lane_lo) * inv_n for u in units]
    bonus = [_group_sum(rs[u] * ks[u] * rk_ref[:, windows[u][1]], lane_lo) * vs[u] for u in units]
    for u in units:
        w = windows[u]
        yn = devs[u] * lax.rsqrt(variances[u] + GN_EPS) * lnw_ref[:, w[1]] + lnb_ref[:, w[1]]
        out_ref[w] = ((yn + bonus[u]) * gate_ref[w]).astype(BF16)


def _rwkv_scan_call(seqs, rk, lnw, lnb, lane_group, chunks_per_step, passes):
    b, tp, d = seqs[0].shape
    c = CHUNK * chunks_per_step
    seq_spec = pl.BlockSpec((None, c, lane_group), lambda bi, l, ci: (bi, ci, l))
    vec_spec = pl.BlockSpec((1, lane_group), lambda bi, l, ci: (0, l))
    return pl.pallas_call(
        functools.partial(_rwkv_scan_kernel, passes=passes),
        grid=(b, d // lane_group, tp // c),
        in_specs=[seq_spec] * 7 + [vec_spec] * 3,
        out_specs=seq_spec,
        out_shape=jax.ShapeDtypeStruct((b, tp, d), BF16),
        scratch_shapes=[pltpu.VMEM((lane_group // LANES, LANES, LANES), F32)],
        compiler_params=pltpu.CompilerParams(
            dimension_semantics=("parallel", "parallel", "arbitrary"),
            vmem_limit_bytes=VMEM_LIMIT),
        name="rwkv_scan",
    )(*seqs, rk, lnw, lnb)


def kernel(x, meta_tokens, attn_norm, w_qkv, w_o_attn, rwkv_norm, rwkv_mu, w_rkv, w_o_rwkv,
           w0, w1, w2, a0, a1, a2, v0, v1, v2, g1, g2, k_k, k_a, r_k, ln_x_w, ln_x_b,
           ffn_norm, w_ffn_in, w_ffn_out, final_norm):
    b, seq, d = x.shape
    n_meta = meta_tokens.shape[0]
    pad = (-n_meta) % ATT_BLOCK
    tp = pad + n_meta + seq
    assert pad + n_meta == ATT_BLOCK and seq % ATT_BLOCK == 0 and d % LANES == 0
    m = b * tp
    blocks_per_seq = tp // ATT_BLOCK
    tm = 512 if m % 512 == 0 else ATT_BLOCK
    tm_prep = 384 if m % 384 == 0 else ATT_BLOCK

    def row(vec):
        return vec.reshape(1, d).astype(F32)

    lead = jnp.concatenate([jnp.zeros((pad, d), x.dtype), meta_tokens.astype(x.dtype)], axis=0)

    qkv, v_first, (wo0, win0, wout0) = _qkv_call(
        x, lead, row(attn_norm[0]), w_qkv[0].astype(BF16), tm, blocks_per_seq,
        [(w_o_attn, 0), (w_ffn_in, 0), (w_ffn_out, 0)])
    o = _attn_call(qkv.reshape(b, tp, 3 * d), pad, 4 * LANES).reshape(m, d)
    h, (wrkv, wo1, win1, wout1) = _mix_ffn_call(
        x, lead, o, wo0, row(ffn_norm[0]), win0, wout0, tm, blocks_per_seq,
        [(w_rkv[0].reshape(1, 3 * d, d), 0), (w_o_rwkv, 0), (w_ffn_in, 1), (w_ffn_out, 1)])

    vecs = jnp.concatenate([row(w0[0]), row(a0[0]), row(v0[0]), row(k_k[0]), row(k_a[0]),
                            jnp.zeros((3, d), F32)], axis=0)
    loras = [w.astype(BF16) for w in (w1[0], w2[0], a1[0], a2[0], v1[0], v2[0], g1[0], g2[0])]
    seqs = _rwkv_prep_call(h, v_first, row(rwkv_norm[0]), rwkv_mu[0], vecs,
                           wrkv.reshape(3, d, d), loras, tm_prep, tp)
    chunks_per_step = 3 if (tp // CHUNK) % 3 == 0 else 2
    y = _rwkv_scan_call([s.reshape(b, tp, d) for s in seqs], row(r_k[0]), row(ln_x_w[0]),
                        row(ln_x_b[0]), 8 * LANES, chunks_per_step, 1)
    tm_out = 512 if seq % 512 == 0 else ATT_BLOCK
    return _mix_ffn_final_call(h.reshape(b, tp, d), y, wo1, row(ffn_norm[1]), win1, wout1,
                               row(final_norm), pad + n_meta, tm_out)
```

```python
import functools

import jax
import jax.numpy as jnp
from jax import lax
from jax.experimental import pallas as pl
from jax.experimental.pallas import tpu as pltpu

F32 = jnp.float32
BF16 = jnp.bfloat16

HEAD_DIM = 64
N_META = 16
ATT_BLOCK = 128
KEY_BLOCKS_PER_TRIP = 2
LANES = 128
BF16_SUBLANES = 16
CHUNK = 64
RMS_EPS = 1e-6
GN_EPS = 64e-5
L2_EPS = 1e-12
EXP2_UNDERFLOW = -127.0
LOG2_E = 1.4426950408889634
VMEM_LIMIT = 56 * 1024 * 1024

_NN = (((1,), (0,)), ((), ()))
_NT = (((1,), (1,)), ((), ()))
_TN = (((0,), (0,)), ((), ()))


def _dg(a, b, dn=_NN):
    return lax.dot_general(a, b, dn, preferred_element_type=F32)


def _split(x, n):
    parts = []
    for _ in range(n):
        p = x.astype(BF16)
        parts.append(p)
        x = x - p.astype(F32)
    return parts


def _mm(a, b, dn=_NN, passes=3):
    if passes == 1:
        return _dg(a.astype(BF16), b.astype(BF16), dn)
    ah, al = _split(a, 2)
    bh, bl = _split(b, 2)
    return _dg(ah, bh, dn) + (_dg(ah, bl, dn) + _dg(al, bh, dn))


def _mm_exact_lhs(a_bf, b, dn=_NN, n=3):
    parts = _split(b, n)
    out = _dg(a_bf, parts[-1], dn)
    for p in parts[-2::-1]:
        out = out + _dg(a_bf, p, dn)
    return out


def _mm_exact_rhs(a, b_bf, dn=_NN, n=3):
    parts = _split(a, n)
    out = _dg(parts[-1], b_bf, dn)
    for p in parts[-2::-1]:
        out = out + _dg(p, b_bf, dn)
    return out


def _rms(x, g):
    return x * lax.rsqrt(jnp.mean(x * x, axis=-1, keepdims=True) + RMS_EPS) * g


def _softplus_neg_abs(z):
    return jnp.log(1.0 + jnp.exp(-jnp.abs(z)))


def _sigmoid(x):
    return 1.0 / (1.0 + jnp.exp(-x))


def _const_spec(shape):
    nd = len(shape)
    return pl.BlockSpec(shape, lambda *_: (0,) * nd, pipeline_mode=pl.Buffered(1))


def _cast_tile_rows(n_rows, n_steps):
    for rows in range(BF16_SUBLANES, n_rows + 1, BF16_SUBLANES):
        if n_rows % rows == 0 and n_rows // rows <= n_steps:
            return rows
    raise ValueError(f"cannot tile {n_rows} rows over {n_steps} steps")


def _cast_specs(weights, n_steps):
    in_specs, out_specs, out_shapes, n_tiles = [], [], [], []
    for w, layer in weights:
        _, n_rows, n_cols = w.shape
        rows = _cast_tile_rows(n_rows, n_steps)
        last = n_rows // rows - 1
        n_tiles.append(last + 1)
        in_specs.append(pl.BlockSpec(
            (None, rows, n_cols), lambda i, layer=layer, last=last: (layer, jnp.minimum(i, last), 0)))
        out_specs.append(pl.BlockSpec(
            (rows, n_cols), lambda i, last=last: (jnp.minimum(i, last), 0)))
        out_shapes.append(jax.ShapeDtypeStruct((n_rows, n_cols), BF16))
    return in_specs, out_specs, out_shapes, tuple(n_tiles)


def _cast_weight_tiles(w_refs, wbf_refs, n_tiles):
    step = pl.program_id(0)
    for w_ref, wbf_ref, n in zip(w_refs, wbf_refs, n_tiles):
        @pl.when(step < n)
        def _(w_ref=w_ref, wbf_ref=wbf_ref):
            wbf_ref[...] = w_ref[...].astype(BF16)


def _padded_piece_specs(pieces, blocks_per_seq, d):
    def spec(q):
        def index_map(i):
            blk_id = i * pieces + q
            return blk_id // blocks_per_seq, jnp.maximum(blk_id % blocks_per_seq - 1, 0), 0
        return pl.BlockSpec((None, ATT_BLOCK, d), index_map)
    return [spec(q) for q in range(pieces)]


def _padded_tile(piece_refs, lead_ref, blocks_per_seq):
    i = pl.program_id(0)
    pieces = len(piece_refs)
    parts = [jnp.where((i * pieces + q) % blocks_per_seq == 0, lead_ref[...], r[...])
             for q, r in enumerate(piece_refs)]
    return jnp.concatenate(parts, axis=0)


def _qkv_kernel(*refs, d, pieces, blocks_per_seq, cast_tiles):
    n_cast = len(cast_tiles)
    lead_ref, g_ref, w_ref = refs[pieces:pieces + 3]
    cast_in = refs[pieces + 3:pieces + 3 + n_cast]
    qkv_ref, v_ref = refs[pieces + 3 + n_cast:pieces + 5 + n_cast]
    _cast_weight_tiles(cast_in, refs[pieces + 5 + n_cast:], cast_tiles)
    x = _padded_tile(refs[:pieces], lead_ref, blocks_per_seq)
    xb = _rms(x, g_ref[...]).astype(BF16)
    for j in range(3):
        y = _dg(xb, w_ref[:, j * d:(j + 1) * d])
        if j == 0:
            y = y * (HEAD_DIM ** -0.5)
        qkv_ref[:, j * d:(j + 1) * d] = y.astype(BF16)
        if j == 2:
            v_ref[...] = y


def _qkv_call(x, lead, g, w_bf, tm, blocks_per_seq, cast_weights):
    b, _, d = x.shape
    m = b * blocks_per_seq * ATT_BLOCK
    pieces = tm // ATT_BLOCK
    c_in, c_out, c_shapes, c_tiles = _cast_specs(cast_weights, m // tm)
    outs = pl.pallas_call(
        functools.partial(_qkv_kernel, d=d, pieces=pieces, blocks_per_seq=blocks_per_seq,
                          cast_tiles=c_tiles),
        grid=(m // tm,),
        in_specs=_padded_piece_specs(pieces, blocks_per_seq, d) + [
            _const_spec((ATT_BLOCK, d)), _const_spec((1, d)), _const_spec((d, 3 * d))] + c_in,
        out_specs=[pl.BlockSpec((tm, 3 * d), lambda i: (i, 0)),
                   pl.BlockSpec((tm, d), lambda i: (i, 0))] + c_out,
        out_shape=[jax.ShapeDtypeStruct((m, 3 * d), BF16),
                   jax.ShapeDtypeStruct((m, d), F32)] + c_shapes,
        compiler_params=pltpu.CompilerParams(
            dimension_semantics=("arbitrary",), vmem_limit_bytes=VMEM_LIMIT),
        name="qkv",
    )(*([x] * pieces), lead, g, w_bf, *[w for w, _ in cast_weights])
    return outs[0], outs[1], outs[2:]


def _attn_kernel(q_ref, k_ref, v_ref, o_ref, *, pad):
    blk = ATT_BLOCK
    n_groups = q_ref.shape[1] // LANES
    i = pl.program_id(2)
    row = lax.broadcasted_iota(jnp.int32, (blk, blk), 0)
    col = lax.broadcasted_iota(jnp.int32, (blk, blk), 1)
    lane_lo = lax.broadcasted_iota(jnp.int32, (blk, LANES), 1) < HEAD_DIM
    q_heads = []
    for p in range(n_groups):
        q = q_ref[:, p * LANES:(p + 1) * LANES]
        zero_q = jnp.zeros_like(q)
        q_heads += [jnp.where(lane_lo, q, zero_q), jnp.where(lane_lo, zero_q, q)]
    upper = (row > col).astype(BF16)

    def cond(carry):
        jj, c_max = carry[0], carry[1]
        return (jj <= i) & (c_max > EXP2_UNDERFLOW)

    def body(carry):
        jj, _, accs, cs = carry
        n_heads = 2 * n_groups
        masks, kts, vts = [], [], []
        for w in range(KEY_BLOCKS_PER_TRIP):
            j = i - jj - w
            start = pl.multiple_of(jnp.maximum(j, 0) * blk, blk)
            s_idx = j * blk + col
            masks.append((s_idx < i * blk + row) & (s_idx >= pad))
            kts.append([k_ref[pl.ds(start, blk), p * LANES:(p + 1) * LANES] for p in range(n_groups)])
            vts.append([v_ref[pl.ds(start, blk), p * LANES:(p + 1) * LANES] for p in range(n_groups)])
        units = [(w, h) for w in range(KEY_BLOCKS_PER_TRIP) for h in range(n_heads)]
        zs = [_dg(q_heads[h], kts[w][h // 2], _NT) * LOG2_E for w, h in units]
        log_beta = [jnp.minimum(z, 0.0) - jnp.log2(1.0 + jnp.exp2(-jnp.abs(z))) for z in zs]
        log_rest = [jnp.where(masks[w], log_beta[u] - zs[u], 0.0)
                    for u, (w, h) in enumerate(units)]
        suffix = [_mm_exact_rhs(lr, upper, n=2) for lr in log_rest]
        row_tot = [jnp.sum(lr, axis=1, keepdims=True) for lr in log_rest]
        carried = list(cs)
        outs = []
        for u, (w, h) in enumerate(units):
            att = jnp.where(masks[w], jnp.exp2(log_beta[u] + suffix[u] + carried[h]), 0.0)
            outs.append(_dg(att.astype(BF16), vts[w][h // 2]))
            carried[h] = carried[h] + row_tot[u]
        new_accs = list(accs)
        for w in range(KEY_BLOCKS_PER_TRIP):
            for p in range(n_groups):
                new_accs[p] = new_accs[p] + jnp.where(lane_lo, outs[w * n_heads + 2 * p],
                                                      outs[w * n_heads + 2 * p + 1])
        c_all = carried[0]
        for c in carried[1:]:
            c_all = jnp.maximum(c_all, c)
        return jj + KEY_BLOCKS_PER_TRIP, jnp.max(c_all), tuple(new_accs), tuple(carried)

    init = (jnp.int32(0), jnp.float32(0.0),
            tuple(jnp.zeros((blk, LANES), F32) for _ in range(n_groups)),
            tuple(jnp.zeros((blk, 1), F32) for _ in range(2 * n_groups)))
    accs = lax.while_loop(cond, body, init)[2]
    for p in range(n_groups):
        o_ref[:, p * LANES:(p + 1) * LANES] = accs[p].astype(BF16)


def _attn_call(qkv3d, pad, lane_set):
    b, tp, d3 = qkv3d.shape
    d = d3 // 3
    ng = d // lane_set
    blk = ATT_BLOCK
    return pl.pallas_call(
        functools.partial(_attn_kernel, pad=pad),
        grid=(b, ng, tp // blk),
        in_specs=[pl.BlockSpec((None, blk, lane_set), lambda bi, p, i: (bi, i, p)),
                  pl.BlockSpec((None, tp, lane_set), lambda bi, p, i: (bi, 0, ng + p)),
                  pl.BlockSpec((None, tp, lane_set), lambda bi, p, i: (bi, 0, 2 * ng + p))],
        out_specs=pl.BlockSpec((None, blk, lane_set), lambda bi, p, i: (bi, i, p)),
        out_shape=jax.ShapeDtypeStruct((b, tp, d), BF16),
        compiler_params=pltpu.CompilerParams(
            dimension_semantics=("parallel", "parallel", "arbitrary"),
            vmem_limit_bytes=VMEM_LIMIT),
        name="attention",
    )(qkv3d, qkv3d, qkv3d)


def _mix_ffn_body(h, o, wo_ref, g_ref, win_ref, wout_ref, f):
    h1 = h + _dg(o, wo_ref[...])
    xb = _rms(h1, g_ref[...]).astype(BF16)
    gu = _dg(xb, win_ref[...])
    gate = gu[:, :f]
    act = (gate * _sigmoid(gate) * gu[:, f:]).astype(BF16)
    return h1 + _dg(act, wout_ref[...])


def _mix_ffn_kernel(*refs, f, pieces, blocks_per_seq, cast_tiles):
    n_cast = len(cast_tiles)
    lead_ref, o_ref, wo_ref, g_ref, win_ref, wout_ref = refs[pieces:pieces + 6]
    cast_in = refs[pieces + 6:pieces + 6 + n_cast]
    out_ref = refs[pieces + 6 + n_cast]
    _cast_weight_tiles(cast_in, refs[pieces + 7 + n_cast:], cast_tiles)
    h = _padded_tile(refs[:pieces], lead_ref, blocks_per_seq)
    out_ref[...] = _mix_ffn_body(h, o_ref[...], wo_ref, g_ref, win_ref, wout_ref, f)


def _mix_ffn_final_kernel(*refs, f, pieces):
    h_refs, o_refs = refs[:pieces], refs[pieces:2 * pieces]
    wo_ref, g_ref, win_ref, wout_ref, fg_ref, out_ref = refs[2 * pieces:]
    h = jnp.concatenate([r[...] for r in h_refs], axis=0)
    o = jnp.concatenate([r[...] for r in o_refs], axis=0)
    out = _mix_ffn_body(h, o, wo_ref, g_ref, win_ref, wout_ref, f)
    out_ref[...] = _rms(out, fg_ref[...])


def _mix_ffn_call(x, lead, o2d, wo_bf, g, win_bf, wout_bf, tm, blocks_per_seq, cast_weights):
    m, d = o2d.shape
    f = wout_bf.shape[0]
    pieces = tm // ATT_BLOCK
    c_in, c_out, c_shapes, c_tiles = _cast_specs(cast_weights, m // tm)
    outs = pl.pallas_call(
        functools.partial(_mix_ffn_kernel, f=f, pieces=pieces, blocks_per_seq=blocks_per_seq,
                          cast_tiles=c_tiles),
        grid=(m // tm,),
        in_specs=_padded_piece_specs(pieces, blocks_per_seq, d) + [
            _const_spec((ATT_BLOCK, d)),
            pl.BlockSpec((tm, d), lambda i: (i, 0)),
            _const_spec((d, d)),
            _const_spec((1, d)),
            _const_spec((d, 2 * f)),
            _const_spec((f, d))] + c_in,
        out_specs=[pl.BlockSpec((tm, d), lambda i: (i, 0))] + c_out,
        out_shape=[jax.ShapeDtypeStruct((m, d), F32)] + c_shapes,
        compiler_params=pltpu.CompilerParams(
            dimension_semantics=("arbitrary",), vmem_limit_bytes=VMEM_LIMIT),
        name="mix_ffn",
    )(*([x] * pieces), lead, o2d, wo_bf, g, win_bf, wout_bf, *[w for w, _ in cast_weights])
    return outs[0], outs[1:]


def _mix_ffn_final_call(h3d, o3d, wo_bf, g, win_bf, wout_bf, final_g, skip, tm):
    b, tp, d = h3d.shape
    f = wout_bf.shape[0]
    blk = ATT_BLOCK
    pieces = tm // blk
    first = skip // blk

    def piece_spec(q):
        return pl.BlockSpec((None, blk, d), lambda bi, j: (bi, first + j * pieces + q, 0))

    in_specs = [piece_spec(q) for q in range(pieces)] * 2
    in_specs += [_const_spec((d, d)), _const_spec((1, d)), _const_spec((d, 2 * f)),
                 _const_spec((f, d)), _const_spec((1, d))]
    return pl.pallas_call(
        functools.partial(_mix_ffn_final_kernel, f=f, pieces=pieces),
        grid=(b, (tp - skip) // tm),
        in_specs=in_specs,
        out_specs=pl.BlockSpec((None, tm, d), lambda bi, j: (bi, j, 0)),
        out_shape=jax.ShapeDtypeStruct((b, tp - skip, d), F32),
        compiler_params=pltpu.CompilerParams(
            dimension_semantics=("parallel", "parallel"), vmem_limit_bytes=VMEM_LIMIT),
        name="mix_ffn_final",
    )(*([h3d] * pieces), *([o3d] * pieces), wo_bf, g, win_bf, wout_bf, final_g)


def _group_sum(x, lane_lo):
    lo = jnp.sum(jnp.where(lane_lo, x, 0.0), axis=1, keepdims=True)
    hi = jnp.sum(jnp.where(lane_lo, 0.0, x), axis=1, keepdims=True)
    return jnp.where(lane_lo, lo, hi)


def _rwkv_prep_kernel(h_ref, hprev_ref, vf_ref, g_ref, mu_ref, vec_ref, wrkv_ref,
                      w1_ref, w2_ref, a1_ref, a2_ref, v1_ref, v2_ref, g1_ref, g2_ref,
                      r_ref, k_ref, v_ref, lw_ref, kk_ref, b_ref, gate_ref, *, tm, tp):
    i = pl.program_id(0)
    g = g_ref[...]
    hn = _rms(h_ref[...], g)
    prev_last = _rms(hprev_ref[...], g)[7:8, :]
    local = lax.broadcasted_iota(jnp.int32, (tm, 1), 0)
    prev = jnp.where(local == 0, prev_last, pltpu.roll(hn, 1, axis=0))
    xx = jnp.where((i * tm + local) % tp == 0, 0.0, prev - hn)

    def mix(n):
        return (hn + xx * mu_ref[n:n + 1, :]).astype(BF16)

    w0, a0, v0 = vec_ref[0:1, :], vec_ref[1:2, :], vec_ref[2:3, :]
    k_k, k_a = vec_ref[3:4, :], vec_ref[4:5, :]

    r_ref[...] = _dg(mix(0), wrkv_ref[0])
    k = _dg(mix(1), wrkv_ref[1])
    xv = mix(2)
    v = _dg(xv, wrkv_ref[2])

    wl = w0 + _dg(jnp.tanh(_dg(mix(3), w1_ref[...])).astype(BF16), w2_ref[...])
    w_log = -(jnp.maximum(-wl, 0.0) + _softplus_neg_abs(wl)) - 0.5
    lw_ref[...] = -jnp.exp(w_log)

    v_gate = _sigmoid(v0 + _dg(_dg(xv, v1_ref[...]).astype(BF16), v2_ref[...]))
    v_ref[...] = v + (vf_ref[...] - v) * v_gate
    a = _sigmoid(a0 + _dg(_dg(mix(4), a1_ref[...]).astype(BF16), a2_ref[...]))
    gate_ref[...] = _dg(_sigmoid(_dg(mix(5), g1_ref[...])).astype(BF16), g2_ref[...])

    lane_lo = lax.broadcasted_iota(jnp.int32, (tm, LANES), 1) < HEAD_DIM
    kk = k * k_k
    for p in range(kk.shape[1] // LANES):
        sl = slice(p * LANES, (p + 1) * LANES)
        kkp = kk[:, sl]
        norm = jnp.maximum(jnp.sqrt(_group_sum(kkp * kkp, lane_lo)), L2_EPS)
        kkp = kkp / norm
        kk_ref[:, sl] = kkp
        b_ref[:, sl] = kkp * a[:, sl]
    k_ref[...] = k * (1.0 + (a - 1.0) * k_a)


def _rwkv_prep_call(h2d, vf2d, g, mu, vecs, wrkv_bf, loras, tm, tp):
    m, d = h2d.shape
    row_spec = pl.BlockSpec((tm, d), lambda i: (i, 0))
    in_specs = [row_spec,
                pl.BlockSpec((8, d), lambda i: (jnp.maximum(i * (tm // 8) - 1, 0), 0)),
                row_spec,
                _const_spec((1, d)), _const_spec(mu.shape), _const_spec(vecs.shape),
                _const_spec(wrkv_bf.shape)]
    in_specs += [_const_spec(w.shape) for w in loras]
    return pl.pallas_call(
        functools.partial(_rwkv_prep_kernel, tm=tm, tp=tp),
        grid=(m // tm,),
        in_specs=in_specs,
        out_specs=[row_spec] * 7,
        out_shape=[jax.ShapeDtypeStruct((m, d), F32)] * 7,
        compiler_params=pltpu.CompilerParams(
            dimension_semantics=("parallel",), vmem_limit_bytes=VMEM_LIMIT),
        name="rwkv_prep",
    )(h2d, h2d, vf2d, g, mu, vecs, wrkv_bf, *loras)


def _stack_heads(x, lane_lo):
    zero = jnp.zeros_like(x)
    return jnp.concatenate([jnp.where(lane_lo, x, zero), jnp.where(lane_lo, zero, x)], axis=0)


def _rwkv_chunk_setup(rs, ks, vs, lws, kks, bs, cums, passes):
    c = CHUNK
    n = 2 * c
    units = range(len(rs))
    lane_lo = lax.broadcasted_iota(jnp.int32, (c, LANES), 1) < HEAD_DIM
    st = functools.partial(_stack_heads, lane_lo=lane_lo)
    row = lax.broadcasted_iota(jnp.int32, (n, n), 0)
    col = lax.broadcasted_iota(jnp.int32, (n, n), 1)
    strict = (col % c) < (row % c)
    incl = (col % c) <= (row % c)
    in_lo = (row < c) & (col < c)
    in_hi = (row >= c) & (col >= c)
    eye = (row == col).astype(F32)

    def mm(a, b, dn=_NN):
        return _mm(a, b, dn, passes)

    def block_diag(g, tri, swap):
        g_sw = pltpu.roll(g, c, axis=1)
        lo, hi = (g_sw, g) if swap else (g, g_sw)
        return jnp.where(tri & in_lo, lo, jnp.where(tri & in_hi, hi, 0.0))

    e_inv = [jnp.exp(-cums[u]) for u in units]
    a_t = [-kks[u] * jnp.exp(cums[u] - lws[u]) for u in units]
    r_t = [rs[u] * jnp.exp(cums[u]) for u in units]
    s_v = [st(vs[u]) for u in units]
    gram = [mm(jnp.concatenate([st(a_t[u]), st(r_t[u])], axis=0),
               jnp.concatenate([bs[u] * e_inv[u], ks[u] * e_inv[u]], axis=0), _NT)
            for u in units]
    a_ab = [block_diag(gram[u][:n], strict, False) for u in units]
    ak_v = [mm(block_diag(gram[u][:n], strict, True), s_v[u]) for u in units]

    inv = [eye + a_ab[u] for u in units]
    power = a_ab
    for _ in range((c - 1).bit_length() - 1):
        power = [mm(power[u], power[u]) for u in units]
        inv = [inv[u] + mm(inv[u], power[u]) for u in units]

    totals = [cums[u][c - 1:c, :] for u in units]
    e_rem = [jnp.exp(totals[u] - cums[u]) for u in units]
    return dict(
        ar=[jnp.concatenate([a_t[u], r_t[u]], axis=0) for u in units],
        s_v=s_v, ak_v=ak_v, inv=inv,
        m_r=[jnp.concatenate([block_diag(gram[u][n:], incl, False),
                              block_diag(gram[u][n:], incl, True)], axis=1) for u in units],
        bk_rem=[jnp.concatenate([st(bs[u] * e_rem[u]), st(ks[u] * e_rem[u])], axis=0)
                for u in units],
        decay=[jnp.exp(totals[u]) for u in units])


def _rwkv_chunk_apply(setup, units, states, passes):
    c = CHUNK
    lane_lo = lax.broadcasted_iota(jnp.int32, (c, LANES), 1) < HEAD_DIM
    st = functools.partial(_stack_heads, lane_lo=lane_lo)
    idx = range(len(units))

    def mm(a, b, dn=_NN):
        return _mm(a, b, dn, passes)

    x1 = [mm(setup["ar"][units[i]], states[i], _NT) for i in idx]
    rhs = [st(x1[i][:c]) + setup["ak_v"][units[i]] for i in idx]
    s_uv = [jnp.concatenate([mm(setup["inv"][units[i]], rhs[i]), setup["s_v"][units[i]]], axis=0)
            for i in idx]
    s_y = [st(x1[i][c:]) + mm(setup["m_r"][units[i]], s_uv[i]) for i in idx]
    ys = [s_y[i][:c] + s_y[i][c:] for i in idx]
    new_states = [states[i] * setup["decay"][units[i]]
                  + mm(s_uv[i], setup["bk_rem"][units[i]], _TN) for i in idx]
    return ys, new_states


def _rwkv_scan_kernel(r_ref, k_ref, v_ref, lw_ref, kk_ref, b_ref, gate_ref,
                      rk_ref, lnw_ref, lnb_ref, out_ref, state_ref, *, passes):
    c = CHUNK
    rows = lw_ref.shape[0]
    n_chunks = rows // c
    n_groups = lw_ref.shape[1] // LANES

    @pl.when(pl.program_id(2) == 0)
    def _():
        state_ref[...] = jnp.zeros_like(state_ref)

    lw = lw_ref[...]
    rowc = lax.broadcasted_iota(jnp.int32, (rows, rows), 0)
    colc = lax.broadcasted_iota(jnp.int32, (rows, rows), 1)
    cum = _mm_exact_lhs(((colc <= rowc) & (colc // c == rowc // c)).astype(BF16), lw, n=3)

    windows = [(slice(ch * c, (ch + 1) * c), slice(p * LANES, (p + 1) * LANES))
               for ch in range(n_chunks) for p in range(n_groups)]
    units = range(len(windows))
    rs = [r_ref[w] for w in windows]
    ks = [k_ref[w] for w in windows]
    vs = [v_ref[w] for w in windows]
    setup = _rwkv_chunk_setup(rs, ks, vs, [lw[w] for w in windows], [kk_ref[w] for w in windows],
                              [b_ref[w] for w in windows], [cum[w] for w in windows], passes)
    states = [state_ref[p] for p in range(n_groups)]
    ys = []
    for ch in range(n_chunks):
        y_ch, states = _rwkv_chunk_apply(
            setup, [ch * n_groups + p for p in range(n_groups)], states, passes)
        ys += y_ch
    for p in range(n_groups):
        state_ref[p] = states[p]

    lane_lo = lax.broadcasted_iota(jnp.int32, (c, LANES), 1) < HEAD_DIM
    inv_n = 1.0 / HEAD_DIM
    means = [_group_sum(ys[u], lane_lo) * inv_n for u in units]
    devs = [ys[u] - means[u] for u in units]
    variances = [_group_sum(devs[u] * devs[u], lane_lo) * inv_n for u in units]
    bonus = [_group_sum(rs[u] * ks[u] * rk_ref[:, windows[u][1]], lane_lo) * vs[u] for u in units]
    for u in units:
        w = windows[u]
        yn = devs[u] * lax.rsqrt(variances[u] + GN_EPS) * lnw_ref[:, w[1]] + lnb_ref[:, w[1]]
        out_ref[w] = ((yn + bonus[u]) * gate_ref[w]).astype(BF16)


def _rwkv_scan_call(seqs, rk, lnw, lnb, lane_group, chunks_per_step, passes):
    b, tp, d = seqs[0].shape
    c = CHUNK * chunks_per_step
    seq_spec = pl.BlockSpec((None, c, lane_group), lambda bi, l, ci: (bi, ci, l))
    vec_spec = pl.BlockSpec((1, lane_group), lambda bi, l, ci: (0, l))
    return pl.pallas_call(
        functools.partial(_rwkv_scan_kernel, passes=passes),
        grid=(b, d // lane_group, tp // c),
        in_specs=[seq_spec] * 7 + [vec_spec] * 3,
        out_specs=seq_spec,
        out_shape=jax.ShapeDtypeStruct((b, tp, d), BF16),
        scratch_shapes=[pltpu.VMEM((lane_group // LANES, LANES, LANES), F32)],
        compiler_params=pltpu.CompilerParams(
            dimension_semantics=("parallel", "parallel", "arbitrary"),
            vmem_limit_bytes=VMEM_LIMIT),
        name="rwkv_scan",
    )(*seqs, rk, lnw, lnb)


def kernel(x, meta_tokens, attn_norm, w_qkv, w_o_attn, rwkv_norm, rwkv_mu, w_rkv, w_o_rwkv,
           w0, w1, w2, a0, a1, a2, v0, v1, v2, g1, g2, k_k, k_a, r_k, ln_x_w, ln_x_b,
           ffn_norm, w_ffn_in, w_ffn_out, final_norm):
    b, seq, d = x.shape
    n_meta = meta_tokens.shape[0]
    pad = (-n_meta) % ATT_BLOCK
    tp = pad + n_meta + seq
    assert pad + n_meta == ATT_BLOCK and seq % ATT_BLOCK == 0 and d % LANES == 0
    m = b * tp
    blocks_per_seq = tp // ATT_BLOCK
    tm = 512 if m % 512 == 0 else ATT_BLOCK
    tm_prep = 384 if m % 384 == 0 else ATT_BLOCK

    def row(vec):
        return vec.reshape(1, d).astype(F32)

    lead = jnp.concatenate([jnp.zeros((pad, d), x.dtype), meta_tokens.astype(x.dtype)], axis=0)

    qkv, v_first, (wo0, win0, wout0) = _qkv_call(
        x, lead, row(attn_norm[0]), w_qkv[0].astype(BF16), tm, blocks_per_seq,
        [(w_o_attn, 0), (w_ffn_in, 0), (w_ffn_out, 0)])
    o = _attn_call(qkv.reshape(b, tp, 3 * d), pad, 8 * LANES).reshape(m, d)
    h, (wrkv, wo1, win1, wout1) = _mix_ffn_call(
        x, lead, o, wo0, row(ffn_norm[0]), win0, wout0, tm, blocks_per_seq,
        [(w_rkv[0].reshape(1, 3 * d, d), 0), (w_o_rwkv, 0), (w_ffn_in, 1), (w_ffn_out, 1)])

    vecs = jnp.concatenate([row(w0[0]), row(a0[0]), row(v0[0]), row(k_k[0]), row(k_a[0]),
                            jnp.zeros((3, d), F32)], axis=0)
    loras = [w.astype(BF16) for w in (w1[0], w2[0], a1[0], a2[0], v1[0], v2[0], g1[0], g2[0])]
    seqs = _rwkv_prep_call(h, v_first, row(rwkv_norm[0]), rwkv_mu[0], vecs,
                           wrkv.reshape(3, d, d), loras, tm_prep, tp)
    chunks_per_step = 3 if (tp // CHUNK) % 3 == 0 else 2
    y = _rwkv_scan_call([s.reshape(b, tp, d) for s in seqs], row(r_k[0]), row(ln_x_w[0]),
                        row(ln_x_b[0]), 8 * LANES, chunks_per_step, 1)
    tm_out = 512 if seq % 512 == 0 else ATT_BLOCK
    return _mix_ffn_final_call(h.reshape(b, tp, d), y, wo1, row(ffn_norm[1]), win1, wout1,
                               row(final_norm), pad + n_meta, tm_out)
```

```python
import functools

import jax
import jax.numpy as jnp
from jax import lax
from jax.experimental import pallas as pl
from jax.experimental.pallas import tpu as pltpu

F32 = jnp.float32
BF16 = jnp.bfloat16

HEAD_DIM = 64
N_META = 16
ATT_BLOCK = 128
KEY_BLOCKS_PER_TRIP = 2
LANES = 128
BF16_SUBLANES = 16
CHUNK = 64
RMS_EPS = 1e-6
GN_EPS = 64e-5
L2_EPS = 1e-12
EXP2_UNDERFLOW = -127.0
LOG2_E = 1.4426950408889634
VMEM_LIMIT = 56 * 1024 * 1024

_NN = (((1,), (0,)), ((), ()))
_NT = (((1,), (1,)), ((), ()))
_TN = (((0,), (0,)), ((), ()))


def _dg(a, b, dn=_NN):
    return lax.dot_general(a, b, dn, preferred_element_type=F32)


def _split(x, n):
    parts = []
    for _ in range(n):
        p = x.astype(BF16)
        parts.append(p)
        x = x - p.astype(F32)
    return parts


def _mm(a, b, dn=_NN, passes=3):
    if passes == 1:
        return _dg(a.astype(BF16), b.astype(BF16), dn)
    ah, al = _split(a, 2)
    bh, bl = _split(b, 2)
    return _dg(ah, bh, dn) + (_dg(ah, bl, dn) + _dg(al, bh, dn))


def _mm_exact_lhs(a_bf, b, dn=_NN, n=3):
    parts = _split(b, n)
    out = _dg(a_bf, parts[-1], dn)
    for p in parts[-2::-1]:
        out = out + _dg(a_bf, p, dn)
    return out


def _mm_exact_rhs(a, b_bf, dn=_NN, n=3):
    parts = _split(a, n)
    out = _dg(parts[-1], b_bf, dn)
    for p in parts[-2::-1]:
        out = out + _dg(p, b_bf, dn)
    return out


def _rms(x, g):
    return x * lax.rsqrt(jnp.mean(x * x, axis=-1, keepdims=True) + RMS_EPS) * g


def _softplus_neg_abs(z):
    return jnp.log(1.0 + jnp.exp(-jnp.abs(z)))


def _sigmoid(x):
    return 1.0 / (1.0 + jnp.exp(-x))


def _const_spec(shape):
    nd = len(shape)
    return pl.BlockSpec(shape, lambda *_: (0,) * nd, pipeline_mode=pl.Buffered(1))


def _cast_tile_rows(n_rows, n_steps):
    for rows in range(BF16_SUBLANES, n_rows + 1, BF16_SUBLANES):
        if n_rows % rows == 0 and n_rows // rows <= n_steps:
            return rows
    raise ValueError(f"cannot tile {n_rows} rows over {n_steps} steps")


def _cast_specs(weights, n_steps):
    in_specs, out_specs, out_shapes, n_tiles = [], [], [], []
    for w, layer in weights:
        _, n_rows, n_cols = w.shape
        rows = _cast_tile_rows(n_rows, n_steps)
        last = n_rows // rows - 1
        n_tiles.append(last + 1)
        in_specs.append(pl.BlockSpec(
            (None, rows, n_cols), lambda i, layer=layer, last=last: (layer, jnp.minimum(i, last), 0)))
        out_specs.append(pl.BlockSpec(
            (rows, n_cols), lambda i, last=last: (jnp.minimum(i, last), 0)))
        out_shapes.append(jax.ShapeDtypeStruct((n_rows, n_cols), BF16))
    return in_specs, out_specs, out_shapes, tuple(n_tiles)


def _cast_weight_tiles(w_refs, wbf_refs, n_tiles):
    step = pl.program_id(0)
    for w_ref, wbf_ref, n in zip(w_refs, wbf_refs, n_tiles):
        @pl.when(step < n)
        def _(w_ref=w_ref, wbf_ref=wbf_ref):
            wbf_ref[...] = w_ref[...].astype(BF16)


def _padded_piece_specs(pieces, blocks_per_seq, d):
    def spec(q):
        def index_map(i):
            blk_id = i * pieces + q
            return blk_id // blocks_per_seq, jnp.maximum(blk_id % blocks_per_seq - 1, 0), 0
        return pl.BlockSpec((None, ATT_BLOCK, d), index_map)
    return [spec(q) for q in range(pieces)]


def _padded_tile(piece_refs, lead_ref, blocks_per_seq):
    i = pl.program_id(0)
    pieces = len(piece_refs)
    parts = [jnp.where((i * pieces + q) % blocks_per_seq == 0, lead_ref[...], r[...])
             for q, r in enumerate(piece_refs)]
    return jnp.concatenate(parts, axis=0)


def _qkv_kernel(*refs, d, pieces, blocks_per_seq, cast_tiles):
    n_cast = len(cast_tiles)
    lead_ref, g_ref, w_ref = refs[pieces:pieces + 3]
    cast_in = refs[pieces + 3:pieces + 3 + n_cast]
    qkv_ref, v_ref = refs[pieces + 3 + n_cast:pieces + 5 + n_cast]
    _cast_weight_tiles(cast_in, refs[pieces + 5 + n_cast:], cast_tiles)
    x = _padded_tile(refs[:pieces], lead_ref, blocks_per_seq)
    xb = _rms(x, g_ref[...]).astype(BF16)
    for j in range(3):
        y = _dg(xb, w_ref[:, j * d:(j + 1) * d])
        if j == 0:
            y = y * (HEAD_DIM ** -0.5)
        qkv_ref[:, j * d:(j + 1) * d] = y.astype(BF16)
        if j == 2:
            v_ref[...] = y


def _qkv_call(x, lead, g, w_bf, tm, blocks_per_seq, cast_weights):
    b, _, d = x.shape
    m = b * blocks_per_seq * ATT_BLOCK
    pieces = tm // ATT_BLOCK
    c_in, c_out, c_shapes, c_tiles = _cast_specs(cast_weights, m // tm)
    outs = pl.pallas_call(
        functools.partial(_qkv_kernel, d=d, pieces=pieces, blocks_per_seq=blocks_per_seq,
                          cast_tiles=c_tiles),
        grid=(m // tm,),
        in_specs=_padded_piece_specs(pieces, blocks_per_seq, d) + [
            _const_spec((ATT_BLOCK, d)), _const_spec((1, d)), _const_spec((d, 3 * d))] + c_in,
        out_specs=[pl.BlockSpec((tm, 3 * d), lambda i: (i, 0)),
                   pl.BlockSpec((tm, d), lambda i: (i, 0))] + c_out,
        out_shape=[jax.ShapeDtypeStruct((m, 3 * d), BF16),
                   jax.ShapeDtypeStruct((m, d), F32)] + c_shapes,
        compiler_params=pltpu.CompilerParams(
            dimension_semantics=("arbitrary",), vmem_limit_bytes=VMEM_LIMIT),
        name="qkv",
    )(*([x] * pieces), lead, g, w_bf, *[w for w, _ in cast_weights])
    return outs[0], outs[1], outs[2:]


def _attn_kernel(q_ref, k_ref, v_ref, o_ref, *, pad):
    blk = ATT_BLOCK
    n_groups = q_ref.shape[1] // LANES
    i = pl.program_id(2)
    row = lax.broadcasted_iota(jnp.int32, (blk, blk), 0)
    col = lax.broadcasted_iota(jnp.int32, (blk, blk), 1)
    lane_lo = lax.broadcasted_iota(jnp.int32, (blk, LANES), 1) < HEAD_DIM
    q_heads = []
    for p in range(n_groups):
        q = q_ref[:, p * LANES:(p + 1) * LANES]
        zero_q = jnp.zeros_like(q)
        q_heads += [jnp.where(lane_lo, q, zero_q), jnp.where(lane_lo, zero_q, q)]
    upper = (row > col).astype(BF16)

    def cond(carry):
        jj, c_max = carry[0], carry[1]
        return (jj <= i) & (c_max > EXP2_UNDERFLOW)

    def body(carry):
        jj, _, accs, cs = carry
        n_heads = 2 * n_groups
        masks, kts, vts = [], [], []
        for w in range(KEY_BLOCKS_PER_TRIP):
            j = i - jj - w
            start = pl.multiple_of(jnp.maximum(j, 0) * blk, blk)
            s_idx = j * blk + col
            masks.append((s_idx < i * blk + row) & (s_idx >= pad))
            kts.append([k_ref[pl.ds(start, blk), p * LANES:(p + 1) * LANES] for p in range(n_groups)])
            vts.append([v_ref[pl.ds(start, blk), p * LANES:(p + 1) * LANES] for p in range(n_groups)])
        units = [(w, h) for w in range(KEY_BLOCKS_PER_TRIP) for h in range(n_heads)]
        zs = [_dg(q_heads[h], kts[w][h // 2], _NT) * LOG2_E for w, h in units]
        log_beta = [jnp.minimum(z, 0.0) - jnp.log2(1.0 + jnp.exp2(-jnp.abs(z))) for z in zs]
        log_rest = [jnp.where(masks[w], log_beta[u] - zs[u], 0.0)
                    for u, (w, h) in enumerate(units)]
        suffix = [_mm_exact_rhs(lr, upper, n=2) for lr in log_rest]
        row_tot = [jnp.sum(lr, axis=1, keepdims=True) for lr in log_rest]
        carried = list(cs)
        outs = []
        for u, (w, h) in enumerate(units):
            att = jnp.where(masks[w], jnp.exp2(log_beta[u] + suffix[u] + carried[h]), 0.0)
            outs.append(_dg(att.astype(BF16), vts[w][h // 2]))
            carried[h] = carried[h] + row_tot[u]
        new_accs = list(accs)
        for w in range(KEY_BLOCKS_PER_TRIP):
            for p in range(n_groups):
                new_accs[p] = new_accs[p] + jnp.where(lane_lo, outs[w * n_heads + 2 * p],
                                                      outs[w * n_heads + 2 * p + 1])
        c_all = carried[0]
        for c in carried[1:]:
            c_all = jnp.maximum(c_all, c)
        return jj + KEY_BLOCKS_PER_TRIP, jnp.max(c_all), tuple(new_accs), tuple(carried)

    init = (jnp.int32(0), jnp.float32(0.0),
            tuple(jnp.zeros((blk, LANES), F32) for _ in range(n_groups)),
            tuple(jnp.zeros((blk, 1), F32) for _ in range(2 * n_groups)))
    accs = lax.while_loop(cond, body, body(init))[2]
    for p in range(n_groups):
        o_ref[:, p * LANES:(p + 1) * LANES] = accs[p].astype(BF16)


def _attn_call(qkv3d, pad, lane_set):
    b, tp, d3 = qkv3d.shape
    d = d3 // 3
    ng = d // lane_set
    blk = ATT_BLOCK
    return pl.pallas_call(
        functools.partial(_attn_kernel, pad=pad),
        grid=(b, ng, tp // blk),
        in_specs=[pl.BlockSpec((None, blk, lane_set), lambda bi, p, i: (bi, i, p)),
                  pl.BlockSpec((None, tp, lane_set), lambda bi, p, i: (bi, 0, ng + p)),
                  pl.BlockSpec((None, tp, lane_set), lambda bi, p, i: (bi, 0, 2 * ng + p))],
        out_specs=pl.BlockSpec((None, blk, lane_set), lambda bi, p, i: (bi, i, p)),
        out_shape=jax.ShapeDtypeStruct((b, tp, d), BF16),
        compiler_params=pltpu.CompilerParams(
            dimension_semantics=("parallel", "parallel", "arbitrary"),
            vmem_limit_bytes=VMEM_LIMIT),
        name="attention",
    )(qkv3d, qkv3d, qkv3d)


def _mix_ffn_body(h, o, wo_ref, g_ref, win_ref, wout_ref, f):
    h1 = h + _dg(o, wo_ref[...])
    xb = _rms(h1, g_ref[...]).astype(BF16)
    gu = _dg(xb, win_ref[...])
    gate = gu[:, :f]
    act = (gate * _sigmoid(gate) * gu[:, f:]).astype(BF16)
    return h1 + _dg(act, wout_ref[...])


def _mix_ffn_kernel(*refs, f, pieces, blocks_per_seq, cast_tiles):
    n_cast = len(cast_tiles)
    lead_ref, o_ref, wo_ref, g_ref, win_ref, wout_ref = refs[pieces:pieces + 6]
    cast_in = refs[pieces + 6:pieces + 6 + n_cast]
    out_ref = refs[pieces + 6 + n_cast]
    _cast_weight_tiles(cast_in, refs[pieces + 7 + n_cast:], cast_tiles)
    h = _padded_tile(refs[:pieces], lead_ref, blocks_per_seq)
    out_ref[...] = _mix_ffn_body(h, o_ref[...], wo_ref, g_ref, win_ref, wout_ref, f)


def _mix_ffn_final_kernel(*refs, f, pieces):
    h_refs, o_refs = refs[:pieces], refs[pieces:2 * pieces]
    wo_ref, g_ref, win_ref, wout_ref, fg_ref, out_ref = refs[2 * pieces:]
    h = jnp.concatenate([r[...] for r in h_refs], axis=0)
    o = jnp.concatenate([r[...] for r in o_refs], axis=0)
    out = _mix_ffn_body(h, o, wo_ref, g_ref, win_ref, wout_ref, f)
    out_ref[...] = _rms(out, fg_ref[...])


def _mix_ffn_call(x, lead, o2d, wo_bf, g, win_bf, wout_bf, tm, blocks_per_seq, cast_weights):
    m, d = o2d.shape
    f = wout_bf.shape[0]
    pieces = tm // ATT_BLOCK
    c_in, c_out, c_shapes, c_tiles = _cast_specs(cast_weights, m // tm)
    outs = pl.pallas_call(
        functools.partial(_mix_ffn_kernel, f=f, pieces=pieces, blocks_per_seq=blocks_per_seq,
                          cast_tiles=c_tiles),
        grid=(m // tm,),
        in_specs=_padded_piece_specs(pieces, blocks_per_seq, d) + [
            _const_spec((ATT_BLOCK, d)),
            pl.BlockSpec((tm, d), lambda i: (i, 0)),
            _const_spec((d, d)),
            _const_spec((1, d)),
            _const_spec((d, 2 * f)),
            _const_spec((f, d))] + c_in,
        out_specs=[pl.BlockSpec((tm, d), lambda i: (i, 0))] + c_out,
        out_shape=[jax.ShapeDtypeStruct((m, d), F32)] + c_shapes,
        compiler_params=pltpu.CompilerParams(
            dimension_semantics=("arbitrary",), vmem_limit_bytes=VMEM_LIMIT),
        name="mix_ffn",
    )(*([x] * pieces), lead, o2d, wo_bf, g, win_bf, wout_bf, *[w for w, _ in cast_weights])
    return outs[0], outs[1:]


def _mix_ffn_final_call(h3d, o3d, wo_bf, g, win_bf, wout_bf, final_g, skip, tm):
    b, tp, d = h3d.shape
    f = wout_bf.shape[0]
    blk = ATT_BLOCK
    pieces = tm // blk
    first = skip // blk

    def piece_spec(q):
        return pl.BlockSpec((None, blk, d), lambda bi, j: (bi, first + j * pieces + q, 0))

    in_specs = [piece_spec(q) for q in range(pieces)] * 2
    in_specs += [_const_spec((d, d)), _const_spec((1, d)), _const_spec((d, 2 * f)),
                 _const_spec((f, d)), _const_spec((1, d))]
    return pl.pallas_call(
        functools.partial(_mix_ffn_final_kernel, f=f, pieces=pieces),
        grid=(b, (tp - skip) // tm),
        in_specs=in_specs,
        out_specs=pl.BlockSpec((None, tm, d), lambda bi, j: (bi, j, 0)),
        out_shape=jax.ShapeDtypeStruct((b, tp - skip, d), F32),
        compiler_params=pltpu.CompilerParams(
            dimension_semantics=("parallel", "parallel"), vmem_limit_bytes=VMEM_LIMIT),
        name="mix_ffn_final",
    )(*([h3d] * pieces), *([o3d] * pieces), wo_bf, g, win_bf, wout_bf, final_g)


def _group_sum(x, lane_lo):
    lo = jnp.sum(jnp.where(lane_lo, x, 0.0), axis=1, keepdims=True)
    hi = jnp.sum(jnp.where(lane_lo, 0.0, x), axis=1, keepdims=True)
    return jnp.where(lane_lo, lo, hi)


def _rwkv_prep_kernel(h_ref, hprev_ref, vf_ref, g_ref, mu_ref, vec_ref, wrkv_ref,
                      w1_ref, w2_ref, a1_ref, a2_ref, v1_ref, v2_ref, g1_ref, g2_ref,
                      r_ref, k_ref, v_ref, lw_ref, kk_ref, b_ref, gate_ref, *, tm, tp):
    i = pl.program_id(0)
    g = g_ref[...]
    hn = _rms(h_ref[...], g)
    prev_last = _rms(hprev_ref[...], g)[7:8, :]
    local = lax.broadcasted_iota(jnp.int32, (tm, 1), 0)
    prev = jnp.where(local == 0, prev_last, pltpu.roll(hn, 1, axis=0))
    xx = jnp.where((i * tm + local) % tp == 0, 0.0, prev - hn)

    def mix(n):
        return (hn + xx * mu_ref[n:n + 1, :]).astype(BF16)

    w0, a0, v0 = vec_ref[0:1, :], vec_ref[1:2, :], vec_ref[2:3, :]
    k_k, k_a = vec_ref[3:4, :], vec_ref[4:5, :]

    r_ref[...] = _dg(mix(0), wrkv_ref[0])
    k = _dg(mix(1), wrkv_ref[1])
    xv = mix(2)
    v = _dg(xv, wrkv_ref[2])

    wl = w0 + _dg(jnp.tanh(_dg(mix(3), w1_ref[...])).astype(BF16), w2_ref[...])
    w_log = -(jnp.maximum(-wl, 0.0) + _softplus_neg_abs(wl)) - 0.5
    lw_ref[...] = -jnp.exp(w_log)

    v_gate = _sigmoid(v0 + _dg(_dg(xv, v1_ref[...]).astype(BF16), v2_ref[...]))
    v_ref[...] = v + (vf_ref[...] - v) * v_gate
    a = _sigmoid(a0 + _dg(_dg(mix(4), a1_ref[...]).astype(BF16), a2_ref[...]))
    gate_ref[...] = _dg(_sigmoid(_dg(mix(5), g1_ref[...])).astype(BF16), g2_ref[...])

    lane_lo = lax.broadcasted_iota(jnp.int32, (tm, LANES), 1) < HEAD_DIM
    kk = k * k_k
    for p in range(kk.shape[1] // LANES):
        sl = slice(p * LANES, (p + 1) * LANES)
        kkp = kk[:, sl]
        norm = jnp.maximum(jnp.sqrt(_group_sum(kkp * kkp, lane_lo)), L2_EPS)
        kkp = kkp / norm
        kk_ref[:, sl] = kkp
        b_ref[:, sl] = kkp * a[:, sl]
    k_ref[...] = k * (1.0 + (a - 1.0) * k_a)


def _rwkv_prep_call(h2d, vf2d, g, mu, vecs, wrkv_bf, loras, tm, tp):
    m, d = h2d.shape
    row_spec = pl.BlockSpec((tm, d), lambda i: (i, 0))
    in_specs = [row_spec,
                pl.BlockSpec((8, d), lambda i: (jnp.maximum(i * (tm // 8) - 1, 0), 0)),
                row_spec,
                _const_spec((1, d)), _const_spec(mu.shape), _const_spec(vecs.shape),
                _const_spec(wrkv_bf.shape)]
    in_specs += [_const_spec(w.shape) for w in loras]
    return pl.pallas_call(
        functools.partial(_rwkv_prep_kernel, tm=tm, tp=tp),
        grid=(m // tm,),
        in_specs=in_specs,
        out_specs=[row_spec] * 7,
        out_shape=[jax.ShapeDtypeStruct((m, d), F32)] * 7,
        compiler_params=pltpu.CompilerParams(
            dimension_semantics=("parallel",), vmem_limit_bytes=VMEM_LIMIT),
        name="rwkv_prep",
    )(h2d, h2d, vf2d, g, mu, vecs, wrkv_bf, *loras)


def _stack_heads(x, lane_lo):
    zero = jnp.zeros_like(x)
    return jnp.concatenate([jnp.where(lane_lo, x, zero), jnp.where(lane_lo, zero, x)], axis=0)


def _rwkv_chunk_setup(rs, ks, vs, lws, kks, bs, cums, passes):
    c = CHUNK
    n = 2 * c
    units = range(len(rs))
    lane_lo = lax.broadcasted_iota(jnp.int32, (c, LANES), 1) < HEAD_DIM
    st = functools.partial(_stack_heads, lane_lo=lane_lo)
    row = lax.broadcasted_iota(jnp.int32, (n, n), 0)
    col = lax.broadcasted_iota(jnp.int32, (n, n), 1)
    strict = (col % c) < (row % c)
    incl = (col % c) <= (row % c)
    in_lo = (row < c) & (col < c)
    in_hi = (row >= c) & (col >= c)
    eye = (row == col).astype(F32)

    def mm(a, b, dn=_NN):
        return _mm(a, b, dn, passes)

    def block_diag(g, tri, swap):
        g_sw = pltpu.roll(g, c, axis=1)
        lo, hi = (g_sw, g) if swap else (g, g_sw)
        return jnp.where(tri & in_lo, lo, jnp.where(tri & in_hi, hi, 0.0))

    e_inv = [jnp.exp(-cums[u]) for u in units]
    a_t = [-kks[u] * jnp.exp(cums[u] - lws[u]) for u in units]
    r_t = [rs[u] * jnp.exp(cums[u]) for u in units]
    s_v = [st(vs[u]) for u in units]
    gram = [mm(jnp.concatenate([st(a_t[u]), st(r_t[u])], axis=0),
               jnp.concatenate([bs[u] * e_inv[u], ks[u] * e_inv[u]], axis=0), _NT)
            for u in units]
    a_ab = [block_diag(gram[u][:n], strict, False) for u in units]
    ak_v = [mm(block_diag(gram[u][:n], strict, True), s_v[u]) for u in units]

    inv = [eye + a_ab[u] for u in units]
    power = a_ab
    for _ in range((c - 1).bit_length() - 1):
        power = [mm(power[u], power[u]) for u in units]
        inv = [inv[u] + mm(inv[u], power[u]) for u in units]

    totals = [cums[u][c - 1:c, :] for u in units]
    e_rem = [jnp.exp(totals[u] - cums[u]) for u in units]
    return dict(
        ar=[jnp.concatenate([a_t[u], r_t[u]], axis=0) for u in units],
        s_v=s_v, ak_v=ak_v, inv=inv,
        m_r=[jnp.concatenate([block_diag(gram[u][n:], incl, False),
                              block_diag(gram[u][n:], incl, True)], axis=1) for u in units],
        bk_rem=[jnp.concatenate([st(bs[u] * e_rem[u]), st(ks[u] * e_rem[u])], axis=0)
                for u in units],
        decay=[jnp.exp(totals[u]) for u in units])


def _rwkv_chunk_apply(setup, units, states, passes):
    c = CHUNK
    lane_lo = lax.broadcasted_iota(jnp.int32, (c, LANES), 1) < HEAD_DIM
    st = functools.partial(_stack_heads, lane_lo=lane_lo)
    idx = range(len(units))

    def mm(a, b, dn=_NN):
        return _mm(a, b, dn, passes)

    x1 = [mm(setup["ar"][units[i]], states[i], _NT) for i in idx]
    rhs = [st(x1[i][:c]) + setup["ak_v"][units[i]] for i in idx]
    s_uv = [jnp.concatenate([mm(setup["inv"][units[i]], rhs[i]), setup["s_v"][units[i]]], axis=0)
            for i in idx]
    s_y = [st(x1[i][c:]) + mm(setup["m_r"][units[i]], s_uv[i]) for i in idx]
    ys = [s_y[i][:c] + s_y[i][c:] for i in idx]
    new_states = [states[i] * setup["decay"][units[i]]
                  + mm(s_uv[i], setup["bk_rem"][units[i]], _TN) for i in idx]
    return ys, new_states


def _rwkv_scan_kernel(r_ref, k_ref, v_ref, lw_ref, kk_ref, b_ref, gate_ref,
                      rk_ref, lnw_ref, lnb_ref, out_ref, state_ref, *, passes):
    c = CHUNK
    rows = lw_ref.shape[0]
    n_chunks = rows // c
    n_groups = lw_ref.shape[1] // LANES

    @pl.when(pl.program_id(2) == 0)
    def _():
        state_ref[...] = jnp.zeros_like(state_ref)

    lw = lw_ref[...]
    rowc = lax.broadcasted_iota(jnp.int32, (rows, rows), 0)
    colc = lax.broadcasted_iota(jnp.int32, (rows, rows), 1)
    cum = _mm_exact_lhs(((colc <= rowc) & (colc // c == rowc // c)).astype(BF16), lw, n=3)

    windows = [(slice(ch * c, (ch + 1) * c), slice(p * LANES, (p + 1) * LANES))
               for ch in range(n_chunks) for p in range(n_groups)]
    units = range(len(windows))
    rs = [r_ref[w] for w in windows]
    ks = [k_ref[w] for w in windows]
    vs = [v_ref[w] for w in windows]
    setup = _rwkv_chunk_setup(rs, ks, vs, [lw[w] for w in windows], [kk_ref[w] for w in windows],
                              [b_ref[w] for w in windows], [cum[w] for w in windows], passes)
    states = [state_ref[p] for p in range(n_groups)]
    ys = []
    for ch in range(n_chunks):
        y_ch, states = _rwkv_chunk_apply(
            setup, [ch * n_groups + p for p in range(n_groups)], states, passes)
        ys += y_ch
    for p in range(n_groups):
        state_ref[p] = states[p]

    lane_lo = lax.broadcasted_iota(jnp.int32, (c, LANES), 1) < HEAD_DIM
    inv_n = 1.0 / HEAD_DIM
    means = [_group_sum(ys[u], lane_lo) * inv_n for u in units]
    devs = [ys[u] - means[u] for u in units]
    variances = [_group_sum(devs[u] * devs[u], lane_lo) * inv_n for u in units]
    bonus = [_group_sum(rs[u] * ks[u] * rk_ref[:, windows[u][1]], lane_lo) * vs[u] for u in units]
    for u in units:
        w = windows[u]
        yn = devs[u] * lax.rsqrt(variances[u] + GN_EPS) * lnw_ref[:, w[1]] + lnb_ref[:, w[1]]
        out_ref[w] = ((yn + bonus[u]) * gate_ref[w]).astype(BF16)


def _rwkv_scan_call(seqs, rk, lnw, lnb, lane_group, chunks_per_step, passes):
    b, tp, d = seqs[0].shape
    c = CHUNK * chunks_per_step
    seq_spec = pl.BlockSpec((None, c, lane_group), lambda bi, l, ci: (bi, ci, l))
    vec_spec = pl.BlockSpec((1, lane_group), lambda bi, l, ci: (0, l))
    return pl.pallas_call(
        functools.partial(_rwkv_scan_kernel, passes=passes),
        grid=(b, d // lane_group, tp // c),
        in_specs=[seq_spec] * 7 + [vec_spec] * 3,
        out_specs=seq_spec,
        out_shape=jax.ShapeDtypeStruct((b, tp, d), BF16),
        scratch_shapes=[pltpu.VMEM((lane_group // LANES, LANES, LANES), F32)],
        compiler_params=pltpu.CompilerParams(
            dimension_semantics=("parallel", "parallel", "arbitrary"),
            vmem_limit_bytes=VMEM_LIMIT),
        name="rwkv_scan",
    )(*seqs, rk, lnw, lnb)


def kernel(x, meta_tokens, attn_norm, w_qkv, w_o_attn, rwkv_norm, rwkv_mu, w_rkv, w_o_rwkv,
           w0, w1, w2, a0, a1, a2, v0, v1, v2, g1, g2, k_k, k_a, r_k, ln_x_w, ln_x_b,
           ffn_norm, w_ffn_in, w_ffn_out, final_norm):
    b, seq, d = x.shape
    n_meta = meta_tokens.shape[0]
    pad = (-n_meta) % ATT_BLOCK
    tp = pad + n_meta + seq
    assert pad + n_meta == ATT_BLOCK and seq % ATT_BLOCK == 0 and d % LANES == 0
    m = b * tp
    blocks_per_seq = tp // ATT_BLOCK
    tm = 512 if m % 512 == 0 else ATT_BLOCK
    tm_prep = 384 if m % 384 == 0 else ATT_BLOCK

    def row(vec):
        return vec.reshape(1, d).astype(F32)

    lead = jnp.concatenate([jnp.zeros((pad, d), x.dtype), meta_tokens.astype(x.dtype)], axis=0)

    qkv, v_first, (wo0, win0, wout0) = _qkv_call(
        x, lead, row(attn_norm[0]), w_qkv[0].astype(BF16), tm, blocks_per_seq,
        [(w_o_attn, 0), (w_ffn_in, 0), (w_ffn_out, 0)])
    o = _attn_call(qkv.reshape(b, tp, 3 * d), pad, 8 * LANES).reshape(m, d)
    h, (wrkv, wo1, win1, wout1) = _mix_ffn_call(
        x, lead, o, wo0, row(ffn_norm[0]), win0, wout0, tm, blocks_per_seq,
        [(w_rkv[0].reshape(1, 3 * d, d), 0), (w_o_rwkv, 0), (w_ffn_in, 1), (w_ffn_out, 1)])

    vecs = jnp.concatenate([row(w0[0]), row(a0[0]), row(v0[0]), row(k_k[0]), row(k_a[0]),
                            jnp.zeros((3, d), F32)], axis=0)
    loras = [w.astype(BF16) for w in (w1[0], w2[0], a1[0], a2[0], v1[0], v2[0], g1[0], g2[0])]
    seqs = _rwkv_prep_call(h, v_first, row(rwkv_norm[0]), rwkv_mu[0], vecs,
                           wrkv.reshape(3, d, d), loras, tm_prep, tp)
    chunks_per_step = 3 if (tp // CHUNK) % 3 == 0 else 2
    y = _rwkv_scan_call([s.reshape(b, tp, d) for s in seqs], row(r_k[0]), row(ln_x_w[0]),
                        row(ln_x_b[0]), 8 * LANES, chunks_per_step, 1)
    tm_out = 512 if seq % 512 == 0 else ATT_BLOCK
    return _mix_ffn_final_call(h.reshape(b, tp, d), y, wo1, row(ffn_norm[1]), win1, wout1,
                               row(final_norm), pad + n_meta, tm_out)
```

```python
import functools

import jax
import jax.numpy as jnp
from jax import lax
from jax.experimental import pallas as pl
from jax.experimental.pallas import tpu as pltpu

F32 = jnp.float32
BF16 = jnp.bfloat16

HEAD_DIM = 64
N_META = 16
ATT_BLOCK = 128
KEY_BLOCKS_PER_TRIP = 2
LANES = 128
BF16_SUBLANES = 16
CHUNK = 64
RMS_EPS = 1e-6
GN_EPS = 64e-5
L2_EPS = 1e-12
EXP2_UNDERFLOW = -127.0
LOG2_E = 1.4426950408889634
VMEM_LIMIT = 56 * 1024 * 1024

_NN = (((1,), (0,)), ((), ()))
_NT = (((1,), (1,)), ((), ()))
_TN = (((0,), (0,)), ((), ()))


def _dg(a, b, dn=_NN):
    return lax.dot_general(a, b, dn, preferred_element_type=F32)


def _split(x, n):
    parts = []
    for _ in range(n):
        p = x.astype(BF16)
        parts.append(p)
        x = x - p.astype(F32)
    return parts


def _mm(a, b, dn=_NN, passes=3):
    if passes == 1:
        return _dg(a.astype(BF16), b.astype(BF16), dn)
    ah, al = _split(a, 2)
    bh, bl = _split(b, 2)
    return _dg(ah, bh, dn) + (_dg(ah, bl, dn) + _dg(al, bh, dn))


def _mm_exact_lhs(a_bf, b, dn=_NN, n=3):
    parts = _split(b, n)
    out = _dg(a_bf, parts[-1], dn)
    for p in parts[-2::-1]:
        out = out + _dg(a_bf, p, dn)
    return out


def _mm_exact_rhs(a, b_bf, dn=_NN, n=3):
    parts = _split(a, n)
    out = _dg(parts[-1], b_bf, dn)
    for p in parts[-2::-1]:
        out = out + _dg(p, b_bf, dn)
    return out


def _rms(x, g):
    return x * lax.rsqrt(jnp.mean(x * x, axis=-1, keepdims=True) + RMS_EPS) * g


def _softplus_neg_abs(z):
    return jnp.log(1.0 + jnp.exp(-jnp.abs(z)))


def _sigmoid(x):
    return 1.0 / (1.0 + jnp.exp(-x))


def _const_spec(shape):
    nd = len(shape)
    return pl.BlockSpec(shape, lambda *_: (0,) * nd, pipeline_mode=pl.Buffered(1))


def _cast_tile_rows(n_rows, n_steps):
    for rows in range(BF16_SUBLANES, n_rows + 1, BF16_SUBLANES):
        if n_rows % rows == 0 and n_rows // rows <= n_steps:
            return rows
    raise ValueError(f"cannot tile {n_rows} rows over {n_steps} steps")


def _cast_specs(weights, n_steps):
    in_specs, out_specs, out_shapes, n_tiles = [], [], [], []
    for w, layer in weights:
        _, n_rows, n_cols = w.shape
        rows = _cast_tile_rows(n_rows, n_steps)
        last = n_rows // rows - 1
        n_tiles.append(last + 1)
        in_specs.append(pl.BlockSpec(
            (None, rows, n_cols), lambda i, layer=layer, last=last: (layer, jnp.minimum(i, last), 0)))
        out_specs.append(pl.BlockSpec(
            (rows, n_cols), lambda i, last=last: (jnp.minimum(i, last), 0)))
        out_shapes.append(jax.ShapeDtypeStruct((n_rows, n_cols), BF16))
    return in_specs, out_specs, out_shapes, tuple(n_tiles)


def _cast_weight_tiles(w_refs, wbf_refs, n_tiles):
    step = pl.program_id(0)
    for w_ref, wbf_ref, n in zip(w_refs, wbf_refs, n_tiles):
        @pl.when(step < n)
        def _(w_ref=w_ref, wbf_ref=wbf_ref):
            wbf_ref[...] = w_ref[...].astype(BF16)


def _padded_piece_specs(pieces, blocks_per_seq, d):
    def spec(q):
        def index_map(i):
            blk_id = i * pieces + q
            return blk_id // blocks_per_seq, jnp.maximum(blk_id % blocks_per_seq - 1, 0), 0
        return pl.BlockSpec((None, ATT_BLOCK, d), index_map)
    return [spec(q) for q in range(pieces)]


def _padded_tile(piece_refs, lead_ref, blocks_per_seq):
    i = pl.program_id(0)
    pieces = len(piece_refs)
    parts = [jnp.where((i * pieces + q) % blocks_per_seq == 0, lead_ref[...], r[...])
             for q, r in enumerate(piece_refs)]
    return jnp.concatenate(parts, axis=0)


def _qkv_kernel(*refs, d, pieces, blocks_per_seq, cast_tiles):
    n_cast = len(cast_tiles)
    lead_ref, g_ref, w_ref = refs[pieces:pieces + 3]
    cast_in = refs[pieces + 3:pieces + 3 + n_cast]
    qkv_ref, v_ref = refs[pieces + 3 + n_cast:pieces + 5 + n_cast]
    _cast_weight_tiles(cast_in, refs[pieces + 5 + n_cast:], cast_tiles)
    x = _padded_tile(refs[:pieces], lead_ref, blocks_per_seq)
    xb = _rms(x, g_ref[...]).astype(BF16)
    for j in range(3):
        y = _dg(xb, w_ref[:, j * d:(j + 1) * d])
        if j == 0:
            y = y * (HEAD_DIM ** -0.5)
        qkv_ref[:, j * d:(j + 1) * d] = y.astype(BF16)
        if j == 2:
            v_ref[...] = y


def _qkv_call(x, lead, g, w_bf, tm, blocks_per_seq, cast_weights):
    b, _, d = x.shape
    m = b * blocks_per_seq * ATT_BLOCK
    pieces = tm // ATT_BLOCK
    c_in, c_out, c_shapes, c_tiles = _cast_specs(cast_weights, m // tm)
    outs = pl.pallas_call(
        functools.partial(_qkv_kernel, d=d, pieces=pieces, blocks_per_seq=blocks_per_seq,
                          cast_tiles=c_tiles),
        grid=(m // tm,),
        in_specs=_padded_piece_specs(pieces, blocks_per_seq, d) + [
            _const_spec((ATT_BLOCK, d)), _const_spec((1, d)), _const_spec((d, 3 * d))] + c_in,
        out_specs=[pl.BlockSpec((tm, 3 * d), lambda i: (i, 0)),
                   pl.BlockSpec((tm, d), lambda i: (i, 0))] + c_out,
        out_shape=[jax.ShapeDtypeStruct((m, 3 * d), BF16),
                   jax.ShapeDtypeStruct((m, d), F32)] + c_shapes,
        compiler_params=pltpu.CompilerParams(
            dimension_semantics=("arbitrary",), vmem_limit_bytes=VMEM_LIMIT),
        name="qkv",
    )(*([x] * pieces), lead, g, w_bf, *[w for w, _ in cast_weights])
    return outs[0], outs[1], outs[2:]


def _attn_kernel(q_ref, k_ref, v_ref, o_ref, *, pad):
    blk = ATT_BLOCK
    n_groups = q_ref.shape[1] // LANES
    q_blocks = q_ref.shape[0] // blk
    row = lax.broadcasted_iota(jnp.int32, (blk, blk), 0)
    col = lax.broadcasted_iota(jnp.int32, (blk, blk), 1)
    lane_lo = lax.broadcasted_iota(jnp.int32, (blk, LANES), 1) < HEAD_DIM
    upper = (row > col).astype(BF16)
    init = (jnp.int32(0), jnp.float32(0.0),
            tuple(jnp.zeros((blk, LANES), F32) for _ in range(n_groups)),
            tuple(jnp.zeros((blk, 1), F32) for _ in range(2 * n_groups)))

    loops = [_attn_query_block(q_ref, k_ref, v_ref, sub, pad, row, col, lane_lo, upper)
             for sub in range(q_blocks)]
    firsts = [body(init) for _, body in loops]
    for sub, ((cond, body), first) in enumerate(zip(loops, firsts)):
        accs = lax.while_loop(cond, body, first)[2]
        for p in range(n_groups):
            o_ref[sub * blk:(sub + 1) * blk, p * LANES:(p + 1) * LANES] = accs[p].astype(BF16)


def _attn_query_block(q_ref, k_ref, v_ref, sub, pad, row, col, lane_lo, upper):
    blk = ATT_BLOCK
    n_groups = q_ref.shape[1] // LANES
    i = pl.program_id(2) * (q_ref.shape[0] // blk) + sub
    q_heads = []
    for p in range(n_groups):
        q = q_ref[sub * blk:(sub + 1) * blk, p * LANES:(p + 1) * LANES]
        zero_q = jnp.zeros_like(q)
        q_heads += [jnp.where(lane_lo, q, zero_q), jnp.where(lane_lo, zero_q, q)]

    def cond(carry):
        jj, c_max = carry[0], carry[1]
        return (jj <= i) & (c_max > EXP2_UNDERFLOW)

    def body(carry):
        jj, _, accs, cs = carry
        n_heads = 2 * n_groups
        masks, kts, vts = [], [], []
        for w in range(KEY_BLOCKS_PER_TRIP):
            j = i - jj - w
            start = pl.multiple_of(jnp.maximum(j, 0) * blk, blk)
            s_idx = j * blk + col
            masks.append((s_idx < i * blk + row) & (s_idx >= pad))
            kts.append([k_ref[pl.ds(start, blk), p * LANES:(p + 1) * LANES] for p in range(n_groups)])
            vts.append([v_ref[pl.ds(start, blk), p * LANES:(p + 1) * LANES] for p in range(n_groups)])
        units = [(w, h) for w in range(KEY_BLOCKS_PER_TRIP) for h in range(n_heads)]
        zs = [_dg(q_heads[h], kts[w][h // 2], _NT) * LOG2_E for w, h in units]
        log_beta = [jnp.minimum(z, 0.0) - jnp.log2(1.0 + jnp.exp2(-jnp.abs(z))) for z in zs]
        log_rest = [jnp.where(masks[w], log_beta[u] - zs[u], 0.0)
                    for u, (w, h) in enumerate(units)]
        suffix = [_mm_exact_rhs(lr, upper, n=2) for lr in log_rest]
        row_tot = [jnp.sum(lr, axis=1, keepdims=True) for lr in log_rest]
        carried = list(cs)
        outs = []
        for u, (w, h) in enumerate(units):
            att = jnp.where(masks[w], jnp.exp2(log_beta[u] + suffix[u] + carried[h]), 0.0)
            outs.append(_dg(att.astype(BF16), vts[w][h // 2]))
            carried[h] = carried[h] + row_tot[u]
        new_accs = list(accs)
        for w in range(KEY_BLOCKS_PER_TRIP):
            for p in range(n_groups):
                new_accs[p] = new_accs[p] + jnp.where(lane_lo, outs[w * n_heads + 2 * p],
                                                      outs[w * n_heads + 2 * p + 1])
        c_all = carried[0]
        for c in carried[1:]:
            c_all = jnp.maximum(c_all, c)
        return jj + KEY_BLOCKS_PER_TRIP, jnp.max(c_all), tuple(new_accs), tuple(carried)

    return cond, body


def _attn_call(qkv3d, pad, lane_set, q_blocks):
    b, tp, d3 = qkv3d.shape
    d = d3 // 3
    ng = d // lane_set
    blk = ATT_BLOCK * q_blocks
    return pl.pallas_call(
        functools.partial(_attn_kernel, pad=pad),
        grid=(b, ng, tp // blk),
        in_specs=[pl.BlockSpec((None, blk, lane_set), lambda bi, p, i: (bi, i, p)),
                  pl.BlockSpec((None, tp, lane_set), lambda bi, p, i: (bi, 0, ng + p)),
                  pl.BlockSpec((None, tp, lane_set), lambda bi, p, i: (bi, 0, 2 * ng + p))],
        out_specs=pl.BlockSpec((None, blk, lane_set), lambda bi, p, i: (bi, i, p)),
        out_shape=jax.ShapeDtypeStruct((b, tp, d), BF16),
        compiler_params=pltpu.CompilerParams(
            dimension_semantics=("parallel", "parallel", "arbitrary"),
            vmem_limit_bytes=VMEM_LIMIT),
        name="attention",
    )(qkv3d, qkv3d, qkv3d)


def _mix_ffn_body(h, o, wo_ref, g_ref, win_ref, wout_ref, f):
    h1 = h + _dg(o, wo_ref[...])
    xb = _rms(h1, g_ref[...]).astype(BF16)
    gu = _dg(xb, win_ref[...])
    gate = gu[:, :f]
    act = (gate * _sigmoid(gate) * gu[:, f:]).astype(BF16)
    return h1 + _dg(act, wout_ref[...])


def _mix_ffn_kernel(*refs, f, pieces, blocks_per_seq, cast_tiles):
    n_cast = len(cast_tiles)
    lead_ref, o_ref, wo_ref, g_ref, win_ref, wout_ref = refs[pieces:pieces + 6]
    cast_in = refs[pieces + 6:pieces + 6 + n_cast]
    out_ref = refs[pieces + 6 + n_cast]
    _cast_weight_tiles(cast_in, refs[pieces + 7 + n_cast:], cast_tiles)
    h = _padded_tile(refs[:pieces], lead_ref, blocks_per_seq)
    out_ref[...] = _mix_ffn_body(h, o_ref[...], wo_ref, g_ref, win_ref, wout_ref, f)


def _mix_ffn_final_kernel(*refs, f, pieces):
    h_refs, o_refs = refs[:pieces], refs[pieces:2 * pieces]
    wo_ref, g_ref, win_ref, wout_ref, fg_ref, out_ref = refs[2 * pieces:]
    h = jnp.concatenate([r[...] for r in h_refs], axis=0)
    o = jnp.concatenate([r[...] for r in o_refs], axis=0)
    out = _mix_ffn_body(h, o, wo_ref, g_ref, win_ref, wout_ref, f)
    out_ref[...] = _rms(out, fg_ref[...])


def _mix_ffn_call(x, lead, o2d, wo_bf, g, win_bf, wout_bf, tm, blocks_per_seq, cast_weights):
    m, d = o2d.shape
    f = wout_bf.shape[0]
    pieces = tm // ATT_BLOCK
    c_in, c_out, c_shapes, c_tiles = _cast_specs(cast_weights, m // tm)
    outs = pl.pallas_call(
        functools.partial(_mix_ffn_kernel, f=f, pieces=pieces, blocks_per_seq=blocks_per_seq,
                          cast_tiles=c_tiles),
        grid=(m // tm,),
        in_specs=_padded_piece_specs(pieces, blocks_per_seq, d) + [
            _const_spec((ATT_BLOCK, d)),
            pl.BlockSpec((tm, d), lambda i: (i, 0)),
            _const_spec((d, d)),
            _const_spec((1, d)),
            _const_spec((d, 2 * f)),
            _const_spec((f, d))] + c_in,
        out_specs=[pl.BlockSpec((tm, d), lambda i: (i, 0))] + c_out,
        out_shape=[jax.ShapeDtypeStruct((m, d), F32)] + c_shapes,
        compiler_params=pltpu.CompilerParams(
            dimension_semantics=("arbitrary",), vmem_limit_bytes=VMEM_LIMIT),
        name="mix_ffn",
    )(*([x] * pieces), lead, o2d, wo_bf, g, win_bf, wout_bf, *[w for w, _ in cast_weights])
    return outs[0], outs[1:]


def _mix_ffn_final_call(h3d, o3d, wo_bf, g, win_bf, wout_bf, final_g, skip, tm):
    b, tp, d = h3d.shape
    f = wout_bf.shape[0]
    blk = ATT_BLOCK
    pieces = tm // blk
    first = skip // blk

    def piece_spec(q):
        return pl.BlockSpec((None, blk, d), lambda bi, j: (bi, first + j * pieces + q, 0))

    in_specs = [piece_spec(q) for q in range(pieces)] * 2
    in_specs += [_const_spec((d, d)), _const_spec((1, d)), _const_spec((d, 2 * f)),
                 _const_spec((f, d)), _const_spec((1, d))]
    return pl.pallas_call(
        functools.partial(_mix_ffn_final_kernel, f=f, pieces=pieces),
        grid=(b, (tp - skip) // tm),
        in_specs=in_specs,
        out_specs=pl.BlockSpec((None, tm, d), lambda bi, j: (bi, j, 0)),
        out_shape=jax.ShapeDtypeStruct((b, tp - skip, d), F32),
        compiler_params=pltpu.CompilerParams(
            dimension_semantics=("parallel", "parallel"), vmem_limit_bytes=VMEM_LIMIT),
        name="mix_ffn_final",
    )(*([h3d] * pieces), *([o3d] * pieces), wo_bf, g, win_bf, wout_bf, final_g)


def _group_sum(x, lane_lo):
    lo = jnp.sum(jnp.where(lane_lo, x, 0.0), axis=1, keepdims=True)
    hi = jnp.sum(jnp.where(lane_lo, 0.0, x), axis=1, keepdims=True)
    return jnp.where(lane_lo, lo, hi)


def _rwkv_prep_kernel(h_ref, hprev_ref, vf_ref, g_ref, mu_ref, vec_ref, wrkv_ref,
                      w1_ref, w2_ref, a1_ref, a2_ref, v1_ref, v2_ref, g1_ref, g2_ref,
                      r_ref, k_ref, v_ref, lw_ref, kk_ref, b_ref, gate_ref, *, tm, tp):
    i = pl.program_id(0)
    g = g_ref[...]
    hn = _rms(h_ref[...], g)
    prev_last = _rms(hprev_ref[...], g)[7:8, :]
    local = lax.broadcasted_iota(jnp.int32, (tm, 1), 0)
    prev = jnp.where(local == 0, prev_last, pltpu.roll(hn, 1, axis=0))
    xx = jnp.where((i * tm + local) % tp == 0, 0.0, prev - hn)

    def mix(n):
        return (hn + xx * mu_ref[n:n + 1, :]).astype(BF16)

    w0, a0, v0 = vec_ref[0:1, :], vec_ref[1:2, :], vec_ref[2:3, :]
    k_k, k_a = vec_ref[3:4, :], vec_ref[4:5, :]

    r_ref[...] = _dg(mix(0), wrkv_ref[0])
    k = _dg(mix(1), wrkv_ref[1])
    xv = mix(2)
    v = _dg(xv, wrkv_ref[2])

    wl = w0 + _dg(jnp.tanh(_dg(mix(3), w1_ref[...])).astype(BF16), w2_ref[...])
    w_log = -(jnp.maximum(-wl, 0.0) + _softplus_neg_abs(wl)) - 0.5
    lw_ref[...] = -jnp.exp(w_log)

    v_gate = _sigmoid(v0 + _dg(_dg(xv, v1_ref[...]).astype(BF16), v2_ref[...]))
    v_ref[...] = v + (vf_ref[...] - v) * v_gate
    a = _sigmoid(a0 + _dg(_dg(mix(4), a1_ref[...]).astype(BF16), a2_ref[...]))
    gate_ref[...] = _dg(_sigmoid(_dg(mix(5), g1_ref[...])).astype(BF16), g2_ref[...])

    lane_lo = lax.broadcasted_iota(jnp.int32, (tm, LANES), 1) < HEAD_DIM
    kk = k * k_k
    for p in range(kk.shape[1] // LANES):
        sl = slice(p * LANES, (p + 1) * LANES)
        kkp = kk[:, sl]
        norm = jnp.maximum(jnp.sqrt(_group_sum(kkp * kkp, lane_lo)), L2_EPS)
        kkp = kkp / norm
        kk_ref[:, sl] = kkp
        b_ref[:, sl] = kkp * a[:, sl]
    k_ref[...] = k * (1.0 + (a - 1.0) * k_a)


def _rwkv_prep_call(h2d, vf2d, g, mu, vecs, wrkv_bf, loras, tm, tp):
    m, d = h2d.shape
    row_spec = pl.BlockSpec((tm, d), lambda i: (i, 0))
    in_specs = [row_spec,
                pl.BlockSpec((8, d), lambda i: (jnp.maximum(i * (tm // 8) - 1, 0), 0)),
                row_spec,
                _const_spec((1, d)), _const_spec(mu.shape), _const_spec(vecs.shape),
                _const_spec(wrkv_bf.shape)]
    in_specs += [_const_spec(w.shape) for w in loras]
    return pl.pallas_call(
        functools.partial(_rwkv_prep_kernel, tm=tm, tp=tp),
        grid=(m // tm,),
        in_specs=in_specs,
        out_specs=[row_spec] * 7,
        out_shape=[jax.ShapeDtypeStruct((m, d), F32)] * 7,
        compiler_params=pltpu.CompilerParams(
            dimension_semantics=("parallel",), vmem_limit_bytes=VMEM_LIMIT),
        name="rwkv_prep",
    )(h2d, h2d, vf2d, g, mu, vecs, wrkv_bf, *loras)


def _stack_heads(x, lane_lo):
    zero = jnp.zeros_like(x)
    return jnp.concatenate([jnp.where(lane_lo, x, zero), jnp.where(lane_lo, zero, x)], axis=0)


def _rwkv_chunk_setup(rs, ks, vs, lws, kks, bs, cums, passes):
    c = CHUNK
    n = 2 * c
    units = range(len(rs))
    lane_lo = lax.broadcasted_iota(jnp.int32, (c, LANES), 1) < HEAD_DIM
    st = functools.partial(_stack_heads, lane_lo=lane_lo)
    row = lax.broadcasted_iota(jnp.int32, (n, n), 0)
    col = lax.broadcasted_iota(jnp.int32, (n, n), 1)
    strict = (col % c) < (row % c)
    incl = (col % c) <= (row % c)
    in_lo = (row < c) & (col < c)
    in_hi = (row >= c) & (col >= c)
    eye = (row == col).astype(F32)

    def mm(a, b, dn=_NN):
        return _mm(a, b, dn, passes)

    def block_diag(g, tri, swap):
        g_sw = pltpu.roll(g, c, axis=1)
        lo, hi = (g_sw, g) if swap else (g, g_sw)
        return jnp.where(tri & in_lo, lo, jnp.where(tri & in_hi, hi, 0.0))

    e_inv = [jnp.exp(-cums[u]) for u in units]
    a_t = [-kks[u] * jnp.exp(cums[u] - lws[u]) for u in units]
    r_t = [rs[u] * jnp.exp(cums[u]) for u in units]
    s_v = [st(vs[u]) for u in units]
    gram = [mm(jnp.concatenate([st(a_t[u]), st(r_t[u])], axis=0),
               jnp.concatenate([bs[u] * e_inv[u], ks[u] * e_inv[u]], axis=0), _NT)
            for u in units]
    a_ab = [block_diag(gram[u][:n], strict, False) for u in units]
    ak_v = [mm(block_diag(gram[u][:n], strict, True), s_v[u]) for u in units]

    inv = [eye + a_ab[u] for u in units]
    power = a_ab
    for _ in range((c - 1).bit_length() - 1):
        power = [mm(power[u], power[u]) for u in units]
        inv = [inv[u] + mm(inv[u], power[u]) for u in units]

    totals = [cums[u][c - 1:c, :] for u in units]
    e_rem = [jnp.exp(totals[u] - cums[u]) for u in units]
    return dict(
        ar=[jnp.concatenate([a_t[u], r_t[u]], axis=0) for u in units],
        s_v=s_v, ak_v=ak_v, inv=inv,
        m_r=[jnp.concatenate([block_diag(gram[u][n:], incl, False),
                              block_diag(gram[u][n:], incl, True)], axis=1) for u in units],
        bk_rem=[jnp.concatenate([st(bs[u] * e_rem[u]), st(ks[u] * e_rem[u])], axis=0)
                for u in units],
        decay=[jnp.exp(totals[u]) for u in units])


def _rwkv_chunk_apply(setup, units, states, passes):
    c = CHUNK
    lane_lo = lax.broadcasted_iota(jnp.int32, (c, LANES), 1) < HEAD_DIM
    st = functools.partial(_stack_heads, lane_lo=lane_lo)
    idx = range(len(units))

    def mm(a, b, dn=_NN):
        return _mm(a, b, dn, passes)

    x1 = [mm(setup["ar"][units[i]], states[i], _NT) for i in idx]
    rhs = [st(x1[i][:c]) + setup["ak_v"][units[i]] for i in idx]
    s_uv = [jnp.concatenate([mm(setup["inv"][units[i]], rhs[i]), setup["s_v"][units[i]]], axis=0)
            for i in idx]
    s_y = [st(x1[i][c:]) + mm(setup["m_r"][units[i]], s_uv[i]) for i in idx]
    ys = [s_y[i][:c] + s_y[i][c:] for i in idx]
    new_states = [states[i] * setup["decay"][units[i]]
                  + mm(s_uv[i], setup["bk_rem"][units[i]], _TN) for i in idx]
    return ys, new_states


def _rwkv_scan_kernel(r_ref, k_ref, v_ref, lw_ref, kk_ref, b_ref, gate_ref,
                      rk_ref, lnw_ref, lnb_ref, out_ref, state_ref, *, passes):
    c = CHUNK
    rows = lw_ref.shape[0]
    n_chunks = rows // c
    n_groups = lw_ref.shape[1] // LANES

    @pl.when(pl.program_id(2) == 0)
    def _():
        state_ref[...] = jnp.zeros_like(state_ref)

    lw = lw_ref[...]
    rowc = lax.broadcasted_iota(jnp.int32, (rows, rows), 0)
    colc = lax.broadcasted_iota(jnp.int32, (rows, rows), 1)
    cum = _mm_exact_lhs(((colc <= rowc) & (colc // c == rowc // c)).astype(BF16), lw, n=3)

    windows = [(slice(ch * c, (ch + 1) * c), slice(p * LANES, (p + 1) * LANES))
               for ch in range(n_chunks) for p in range(n_groups)]
    units = range(len(windows))
    rs = [r_ref[w] for w in windows]
    ks = [k_ref[w] for w in windows]
    vs = [v_ref[w] for w in windows]
    setup = _rwkv_chunk_setup(rs, ks, vs, [lw[w] for w in windows], [kk_ref[w] for w in windows],
                              [b_ref[w] for w in windows], [cum[w] for w in windows], passes)
    states = [state_ref[p] for p in range(n_groups)]
    ys = []
    for ch in range(n_chunks):
        y_ch, states = _rwkv_chunk_apply(
            setup, [ch * n_groups + p for p in range(n_groups)], states, passes)
        ys += y_ch
    for p in range(n_groups):
        state_ref[p] = states[p]

    lane_lo = lax.broadcasted_iota(jnp.int32, (c, LANES), 1) < HEAD_DIM
    inv_n = 1.0 / HEAD_DIM
    means = [_group_sum(ys[u], lane_lo) * inv_n for u in units]
    devs = [ys[u] - means[u] for u in units]
    variances = [_group_sum(devs[u] * devs[u], lane_lo) * inv_n for u in units]
    bonus = [_group_sum(rs[u] * ks[u] * rk_ref[:, windows[u][1]], lane_lo) * vs[u] for u in units]
    for u in units:
        w = windows[u]
        yn = devs[u] * lax.rsqrt(variances[u] + GN_EPS) * lnw_ref[:, w[1]] + lnb_ref[:, w[1]]
        out_ref[w] = ((yn + bonus[u]) * gate_ref[w]).astype(BF16)


def _rwkv_scan_call(seqs, rk, lnw, lnb, lane_group, chunks_per_step, passes):
    b, tp, d = seqs[0].shape
    c = CHUNK * chunks_per_step
    seq_spec = pl.BlockSpec((None, c, lane_group), lambda bi, l, ci: (bi, ci, l))
    vec_spec = pl.BlockSpec((1, lane_group), lambda bi, l, ci: (0, l))
    return pl.pallas_call(
        functools.partial(_rwkv_scan_kernel, passes=passes),
        grid=(b, d // lane_group, tp // c),
        in_specs=[seq_spec] * 7 + [vec_spec] * 3,
        out_specs=seq_spec,
        out_shape=jax.ShapeDtypeStruct((b, tp, d), BF16),
        scratch_shapes=[pltpu.VMEM((lane_group // LANES, LANES, LANES), F32)],
        compiler_params=pltpu.CompilerParams(
            dimension_semantics=("parallel", "parallel", "arbitrary"),
            vmem_limit_bytes=VMEM_LIMIT),
        name="rwkv_scan",
    )(*seqs, rk, lnw, lnb)


def kernel(x, meta_tokens, attn_norm, w_qkv, w_o_attn, rwkv_norm, rwkv_mu, w_rkv, w_o_rwkv,
           w0, w1, w2, a0, a1, a2, v0, v1, v2, g1, g2, k_k, k_a, r_k, ln_x_w, ln_x_b,
           ffn_norm, w_ffn_in, w_ffn_out, final_norm):
    b, seq, d = x.shape
    n_meta = meta_tokens.shape[0]
    pad = (-n_meta) % ATT_BLOCK
    tp = pad + n_meta + seq
    assert pad + n_meta == ATT_BLOCK and seq % ATT_BLOCK == 0 and d % LANES == 0
    m = b * tp
    blocks_per_seq = tp // ATT_BLOCK
    tm = 512 if m % 512 == 0 else ATT_BLOCK
    tm_prep = 384 if m % 384 == 0 else ATT_BLOCK

    def row(vec):
        return vec.reshape(1, d).astype(F32)

    lead = jnp.concatenate([jnp.zeros((pad, d), x.dtype), meta_tokens.astype(x.dtype)], axis=0)

    qkv, v_first, (wo0, win0, wout0) = _qkv_call(
        x, lead, row(attn_norm[0]), w_qkv[0].astype(BF16), tm, blocks_per_seq,
        [(w_o_attn, 0), (w_ffn_in, 0), (w_ffn_out, 0)])
    q_blocks = 3 if blocks_per_seq % 3 == 0 else 1
    o = _attn_call(qkv.reshape(b, tp, 3 * d), pad, 4 * LANES, q_blocks).reshape(m, d)
    h, (wrkv, wo1, win1, wout1) = _mix_ffn_call(
        x, lead, o, wo0, row(ffn_norm[0]), win0, wout0, tm, blocks_per_seq,
        [(w_rkv[0].reshape(1, 3 * d, d), 0), (w_o_rwkv, 0), (w_ffn_in, 1), (w_ffn_out, 1)])

    vecs = jnp.concatenate([row(w0[0]), row(a0[0]), row(v0[0]), row(k_k[0]), row(k_a[0]),
                            jnp.zeros((3, d), F32)], axis=0)
    loras = [w.astype(BF16) for w in (w1[0], w2[0], a1[0], a2[0], v1[0], v2[0], g1[0], g2[0])]
    seqs = _rwkv_prep_call(h, v_first, row(rwkv_norm[0]), rwkv_mu[0], vecs,
                           wrkv.reshape(3, d, d), loras, tm_prep, tp)
    chunks_per_step = 3 if (tp // CHUNK) % 3 == 0 else 2
    y = _rwkv_scan_call([s.reshape(b, tp, d) for s in seqs], row(r_k[0]), row(ln_x_w[0]),
                        row(ln_x_b[0]), 8 * LANES, chunks_per_step, 1)
    tm_out = 512 if seq % 512 == 0 else ATT_BLOCK
    return _mix_ffn_final_call(h.reshape(b, tp, d), y, wo1, row(ffn_norm[1]), win1, wout1,
                               row(final_norm), pad + n_meta, tm_out)
```

```python
import functools

import jax
import jax.numpy as jnp
from jax import lax
from jax.experimental import pallas as pl
from jax.experimental.pallas import tpu as pltpu

F32 = jnp.float32
BF16 = jnp.bfloat16

HEAD_DIM = 64
N_META = 16
ATT_BLOCK = 128
KEY_BLOCKS_PER_TRIP = 2
LANES = 128
BF16_SUBLANES = 16
CHUNK = 64
RMS_EPS = 1e-6
GN_EPS = 64e-5
L2_EPS = 1e-12
EXP2_UNDERFLOW = -127.0
LOG2_E = 1.4426950408889634
VMEM_LIMIT = 56 * 1024 * 1024

_NN = (((1,), (0,)), ((), ()))
_NT = (((1,), (1,)), ((), ()))
_TN = (((0,), (0,)), ((), ()))


def _dg(a, b, dn=_NN):
    return lax.dot_general(a, b, dn, preferred_element_type=F32)


def _split(x, n):
    parts = []
    for _ in range(n):
        p = x.astype(BF16)
        parts.append(p)
        x = x - p.astype(F32)
    return parts


def _mm(a, b, dn=_NN, passes=3):
    if passes == 1:
        return _dg(a.astype(BF16), b.astype(BF16), dn)
    ah, al = _split(a, 2)
    bh, bl = _split(b, 2)
    return _dg(ah, bh, dn) + (_dg(ah, bl, dn) + _dg(al, bh, dn))


def _mm_exact_lhs(a_bf, b, dn=_NN, n=3):
    parts = _split(b, n)
    out = _dg(a_bf, parts[-1], dn)
    for p in parts[-2::-1]:
        out = out + _dg(a_bf, p, dn)
    return out


def _mm_exact_rhs(a, b_bf, dn=_NN, n=3):
    parts = _split(a, n)
    out = _dg(parts[-1], b_bf, dn)
    for p in parts[-2::-1]:
        out = out + _dg(p, b_bf, dn)
    return out


def _rms(x, g):
    return x * lax.rsqrt(jnp.mean(x * x, axis=-1, keepdims=True) + RMS_EPS) * g


def _softplus_neg_abs(z):
    return jnp.log(1.0 + jnp.exp(-jnp.abs(z)))


def _sigmoid(x):
    return 1.0 / (1.0 + jnp.exp(-x))


def _const_spec(shape):
    nd = len(shape)
    return pl.BlockSpec(shape, lambda *_: (0,) * nd, pipeline_mode=pl.Buffered(1))


def _cast_tile_rows(n_rows, n_steps):
    for rows in range(BF16_SUBLANES, n_rows + 1, BF16_SUBLANES):
        if n_rows % rows == 0 and n_rows // rows <= n_steps:
            return rows
    raise ValueError(f"cannot tile {n_rows} rows over {n_steps} steps")


def _cast_specs(weights, n_steps):
    in_specs, out_specs, out_shapes, n_tiles = [], [], [], []
    for w, layer in weights:
        _, n_rows, n_cols = w.shape
        rows = _cast_tile_rows(n_rows, n_steps)
        last = n_rows // rows - 1
        n_tiles.append(last + 1)
        in_specs.append(pl.BlockSpec(
            (None, rows, n_cols), lambda i, layer=layer, last=last: (layer, jnp.minimum(i, last), 0)))
        out_specs.append(pl.BlockSpec(
            (rows, n_cols), lambda i, last=last: (jnp.minimum(i, last), 0)))
        out_shapes.append(jax.ShapeDtypeStruct((n_rows, n_cols), BF16))
    return in_specs, out_specs, out_shapes, tuple(n_tiles)


def _cast_weight_tiles(w_refs, wbf_refs, n_tiles):
    step = pl.program_id(0)
    for w_ref, wbf_ref, n in zip(w_refs, wbf_refs, n_tiles):
        @pl.when(step < n)
        def _(w_ref=w_ref, wbf_ref=wbf_ref):
            wbf_ref[...] = w_ref[...].astype(BF16)


def _padded_piece_specs(pieces, blocks_per_seq, d):
    def spec(q):
        def index_map(i):
            blk_id = i * pieces + q
            return blk_id // blocks_per_seq, jnp.maximum(blk_id % blocks_per_seq - 1, 0), 0
        return pl.BlockSpec((None, ATT_BLOCK, d), index_map)
    return [spec(q) for q in range(pieces)]


def _padded_tile(piece_refs, lead_ref, blocks_per_seq):
    i = pl.program_id(0)
    pieces = len(piece_refs)
    parts = [jnp.where((i * pieces + q) % blocks_per_seq == 0, lead_ref[...], r[...])
             for q, r in enumerate(piece_refs)]
    return jnp.concatenate(parts, axis=0)


def _qkv_kernel(*refs, d, pieces, blocks_per_seq, cast_tiles):
    n_cast = len(cast_tiles)
    lead_ref, g_ref, w_ref = refs[pieces:pieces + 3]
    cast_in = refs[pieces + 3:pieces + 3 + n_cast]
    qkv_ref, v_ref = refs[pieces + 3 + n_cast:pieces + 5 + n_cast]
    _cast_weight_tiles(cast_in, refs[pieces + 5 + n_cast:], cast_tiles)
    x = _padded_tile(refs[:pieces], lead_ref, blocks_per_seq)
    xb = _rms(x, g_ref[...]).astype(BF16)
    for j in range(3):
        y = _dg(xb, w_ref[:, j * d:(j + 1) * d])
        if j == 0:
            y = y * (HEAD_DIM ** -0.5)
        qkv_ref[:, j * d:(j + 1) * d] = y.astype(BF16)
        if j == 2:
            v_ref[...] = y


def _qkv_call(x, lead, g, w_bf, tm, blocks_per_seq, cast_weights):
    b, _, d = x.shape
    m = b * blocks_per_seq * ATT_BLOCK
    pieces = tm // ATT_BLOCK
    c_in, c_out, c_shapes, c_tiles = _cast_specs(cast_weights, m // tm)
    outs = pl.pallas_call(
        functools.partial(_qkv_kernel, d=d, pieces=pieces, blocks_per_seq=blocks_per_seq,
                          cast_tiles=c_tiles),
        grid=(m // tm,),
        in_specs=_padded_piece_specs(pieces, blocks_per_seq, d) + [
            _const_spec((ATT_BLOCK, d)), _const_spec((1, d)), _const_spec((d, 3 * d))] + c_in,
        out_specs=[pl.BlockSpec((tm, 3 * d), lambda i: (i, 0)),
                   pl.BlockSpec((tm, d), lambda i: (i, 0))] + c_out,
        out_shape=[jax.ShapeDtypeStruct((m, 3 * d), BF16),
                   jax.ShapeDtypeStruct((m, d), F32)] + c_shapes,
        compiler_params=pltpu.CompilerParams(
            dimension_semantics=("arbitrary",), vmem_limit_bytes=VMEM_LIMIT),
        name="qkv",
    )(*([x] * pieces), lead, g, w_bf, *[w for w, _ in cast_weights])
    return outs[0], outs[1], outs[2:]


def _attn_kernel(q_ref, k_ref, v_ref, o_ref, *, pad):
    blk = ATT_BLOCK
    n_groups = q_ref.shape[1] // LANES
    q_blocks = q_ref.shape[0] // blk
    row = lax.broadcasted_iota(jnp.int32, (blk, blk), 0)
    col = lax.broadcasted_iota(jnp.int32, (blk, blk), 1)
    lane_lo = lax.broadcasted_iota(jnp.int32, (blk, LANES), 1) < HEAD_DIM
    upper = (row > col).astype(BF16)
    init = (jnp.int32(0), jnp.float32(0.0),
            tuple(jnp.zeros((blk, LANES), F32) for _ in range(n_groups)),
            tuple(jnp.zeros((blk, 1), F32) for _ in range(2 * n_groups)))

    loops = [_attn_query_block(q_ref, k_ref, v_ref, sub, pad, row, col, lane_lo, upper)
             for sub in range(q_blocks)]
    firsts = [body(init) for _, body in loops]
    for sub, ((cond, body), first) in enumerate(zip(loops, firsts)):
        accs = lax.while_loop(cond, body, first)[2]
        for p in range(n_groups):
            o_ref[sub * blk:(sub + 1) * blk, p * LANES:(p + 1) * LANES] = accs[p].astype(BF16)


def _attn_query_block(q_ref, k_ref, v_ref, sub, pad, row, col, lane_lo, upper):
    blk = ATT_BLOCK
    n_groups = q_ref.shape[1] // LANES
    i = pl.program_id(2) * (q_ref.shape[0] // blk) + sub
    q_heads = []
    for p in range(n_groups):
        q = q_ref[sub * blk:(sub + 1) * blk, p * LANES:(p + 1) * LANES]
        zero_q = jnp.zeros_like(q)
        q_heads += [jnp.where(lane_lo, q, zero_q), jnp.where(lane_lo, zero_q, q)]

    def cond(carry):
        jj, c_max = carry[0], carry[1]
        return (jj <= i) & (c_max > EXP2_UNDERFLOW)

    def body(carry):
        jj, _, accs, cs = carry
        n_heads = 2 * n_groups
        masks, kts, vts = [], [], []
        for w in range(KEY_BLOCKS_PER_TRIP):
            j = i - jj - w
            start = pl.multiple_of(jnp.maximum(j, 0) * blk, blk)
            s_idx = j * blk + col
            masks.append((s_idx < i * blk + row) & (s_idx >= pad))
            kts.append([k_ref[pl.ds(start, blk), p * LANES:(p + 1) * LANES] for p in range(n_groups)])
            vts.append([v_ref[pl.ds(start, blk), p * LANES:(p + 1) * LANES] for p in range(n_groups)])
        units = [(w, h) for w in range(KEY_BLOCKS_PER_TRIP) for h in range(n_heads)]
        zs = [_dg(q_heads[h], kts[w][h // 2], _NT) * LOG2_E for w, h in units]
        log_beta = [jnp.minimum(z, 0.0) - jnp.log2(1.0 + jnp.exp2(-jnp.abs(z))) for z in zs]
        log_rest = [jnp.where(masks[w], log_beta[u] - zs[u], 0.0)
                    for u, (w, h) in enumerate(units)]
        suffix = [_mm_exact_rhs(lr, upper, n=2) for lr in log_rest]
        row_tot = [jnp.sum(lr, axis=1, keepdims=True) for lr in log_rest]
        carried = list(cs)
        outs = []
        for u, (w, h) in enumerate(units):
            att = jnp.where(masks[w], jnp.exp2(log_beta[u] + suffix[u] + carried[h]), 0.0)
            outs.append(_dg(att.astype(BF16), vts[w][h // 2]))
            carried[h] = carried[h] + row_tot[u]
        new_accs = list(accs)
        for w in range(KEY_BLOCKS_PER_TRIP):
            for p in range(n_groups):
                new_accs[p] = new_accs[p] + jnp.where(lane_lo, outs[w * n_heads + 2 * p],
                                                      outs[w * n_heads + 2 * p + 1])
        c_all = carried[0]
        for c in carried[1:]:
            c_all = jnp.maximum(c_all, c)
        return jj + KEY_BLOCKS_PER_TRIP, jnp.max(c_all), tuple(new_accs), tuple(carried)

    return cond, body


def _attn_call(qkv3d, pad, lane_set, q_blocks):
    b, tp, d3 = qkv3d.shape
    d = d3 // 3
    ng = d // lane_set
    blk = ATT_BLOCK * q_blocks
    return pl.pallas_call(
        functools.partial(_attn_kernel, pad=pad),
        grid=(b, ng, tp // blk),
        in_specs=[pl.BlockSpec((None, blk, lane_set), lambda bi, p, i: (bi, i, p)),
                  pl.BlockSpec((None, tp, lane_set), lambda bi, p, i: (bi, 0, ng + p)),
                  pl.BlockSpec((None, tp, lane_set), lambda bi, p, i: (bi, 0, 2 * ng + p))],
        out_specs=pl.BlockSpec((None, blk, lane_set), lambda bi, p, i: (bi, i, p)),
        out_shape=jax.ShapeDtypeStruct((b, tp, d), BF16),
        compiler_params=pltpu.CompilerParams(
            dimension_semantics=("parallel", "parallel", "arbitrary"),
            vmem_limit_bytes=VMEM_LIMIT),
        name="attention",
    )(qkv3d, qkv3d, qkv3d)


def _mix_ffn_body(h, o, wo_ref, g_ref, win_ref, wout_ref, f):
    h1 = h + _dg(o, wo_ref[...])
    xb = _rms(h1, g_ref[...]).astype(BF16)
    gu = _dg(xb, win_ref[...])
    gate = gu[:, :f]
    act = (gate * _sigmoid(gate) * gu[:, f:]).astype(BF16)
    return h1 + _dg(act, wout_ref[...])


def _mix_ffn_kernel(*refs, f, pieces, blocks_per_seq, cast_tiles):
    n_cast = len(cast_tiles)
    lead_ref, o_ref, wo_ref, g_ref, win_ref, wout_ref = refs[pieces:pieces + 6]
    cast_in = refs[pieces + 6:pieces + 6 + n_cast]
    out_ref = refs[pieces + 6 + n_cast]
    _cast_weight_tiles(cast_in, refs[pieces + 7 + n_cast:], cast_tiles)
    h = _padded_tile(refs[:pieces], lead_ref, blocks_per_seq)
    out_ref[...] = _mix_ffn_body(h, o_ref[...], wo_ref, g_ref, win_ref, wout_ref, f)


def _mix_ffn_final_kernel(*refs, f, pieces):
    h_refs, o_refs = refs[:pieces], refs[pieces:2 * pieces]
    wo_ref, g_ref, win_ref, wout_ref, fg_ref, out_ref = refs[2 * pieces:]
    h = jnp.concatenate([r[...] for r in h_refs], axis=0)
    o = jnp.concatenate([r[...] for r in o_refs], axis=0)
    out = _mix_ffn_body(h, o, wo_ref, g_ref, win_ref, wout_ref, f)
    out_ref[...] = _rms(out, fg_ref[...])


def _mix_ffn_call(x, lead, o2d, wo_bf, g, win_bf, wout_bf, tm, blocks_per_seq, cast_weights):
    m, d = o2d.shape
    f = wout_bf.shape[0]
    pieces = tm // ATT_BLOCK
    c_in, c_out, c_shapes, c_tiles = _cast_specs(cast_weights, m // tm)
    outs = pl.pallas_call(
        functools.partial(_mix_ffn_kernel, f=f, pieces=pieces, blocks_per_seq=blocks_per_seq,
                          cast_tiles=c_tiles),
        grid=(m // tm,),
        in_specs=_padded_piece_specs(pieces, blocks_per_seq, d) + [
            _const_spec((ATT_BLOCK, d)),
            pl.BlockSpec((tm, d), lambda i: (i, 0)),
            _const_spec((d, d)),
            _const_spec((1, d)),
            _const_spec((d, 2 * f)),
            _const_spec((f, d))] + c_in,
        out_specs=[pl.BlockSpec((tm, d), lambda i: (i, 0))] + c_out,
        out_shape=[jax.ShapeDtypeStruct((m, d), F32)] + c_shapes,
        compiler_params=pltpu.CompilerParams(
            dimension_semantics=("arbitrary",), vmem_limit_bytes=VMEM_LIMIT),
        name="mix_ffn",
    )(*([x] * pieces), lead, o2d, wo_bf, g, win_bf, wout_bf, *[w for w, _ in cast_weights])
    return outs[0], outs[1:]


def _mix_ffn_final_call(h3d, o3d, wo_bf, g, win_bf, wout_bf, final_g, skip, tm):
    b, tp, d = h3d.shape
    f = wout_bf.shape[0]
    blk = ATT_BLOCK
    pieces = tm // blk
    first = skip // blk

    def piece_spec(q):
        return pl.BlockSpec((None, blk, d), lambda bi, j: (bi, first + j * pieces + q, 0))

    in_specs = [piece_spec(q) for q in range(pieces)] * 2
    in_specs += [_const_spec((d, d)), _const_spec((1, d)), _const_spec((d, 2 * f)),
                 _const_spec((f, d)), _const_spec((1, d))]
    return pl.pallas_call(
        functools.partial(_mix_ffn_final_kernel, f=f, pieces=pieces),
        grid=(b, (tp - skip) // tm),
        in_specs=in_specs,
        out_specs=pl.BlockSpec((None, tm, d), lambda bi, j: (bi, j, 0)),
        out_shape=jax.ShapeDtypeStruct((b, tp - skip, d), F32),
        compiler_params=pltpu.CompilerParams(
            dimension_semantics=("parallel", "parallel"), vmem_limit_bytes=VMEM_LIMIT),
        name="mix_ffn_final",
    )(*([h3d] * pieces), *([o3d] * pieces), wo_bf, g, win_bf, wout_bf, final_g)


def _group_sum(x, lane_lo):
    lo = jnp.sum(jnp.where(lane_lo, x, 0.0), axis=1, keepdims=True)
    hi = jnp.sum(jnp.where(lane_lo, 0.0, x), axis=1, keepdims=True)
    return jnp.where(lane_lo, lo, hi)


def _rwkv_prep_kernel(h_ref, hprev_ref, vf_ref, g_ref, mu_ref, vec_ref, wrkv_ref,
                      w1_ref, w2_ref, a1_ref, a2_ref, v1_ref, v2_ref, g1_ref, g2_ref,
                      r_ref, k_ref, v_ref, lw_ref, kk_ref, b_ref, gate_ref, *, tm, tp):
    i = pl.program_id(0)
    g = g_ref[...]
    hn = _rms(h_ref[...], g)
    prev_last = _rms(hprev_ref[...], g)[7:8, :]
    local = lax.broadcasted_iota(jnp.int32, (tm, 1), 0)
    prev = jnp.where(local == 0, prev_last, pltpu.roll(hn, 1, axis=0))
    xx = jnp.where((i * tm + local) % tp == 0, 0.0, prev - hn)

    def mix(n):
        return (hn + xx * mu_ref[n:n + 1, :]).astype(BF16)

    w0, a0, v0 = vec_ref[0:1, :], vec_ref[1:2, :], vec_ref[2:3, :]
    k_k, k_a = vec_ref[3:4, :], vec_ref[4:5, :]

    r_ref[...] = _dg(mix(0), wrkv_ref[0])
    k = _dg(mix(1), wrkv_ref[1])
    xv = mix(2)
    v = _dg(xv, wrkv_ref[2])

    wl = w0 + _dg(jnp.tanh(_dg(mix(3), w1_ref[...])).astype(BF16), w2_ref[...])
    w_log = -(jnp.maximum(-wl, 0.0) + _softplus_neg_abs(wl)) - 0.5
    lw_ref[...] = -jnp.exp(w_log)

    v_gate = _sigmoid(v0 + _dg(_dg(xv, v1_ref[...]).astype(BF16), v2_ref[...]))
    v_ref[...] = v + (vf_ref[...] - v) * v_gate
    a = _sigmoid(a0 + _dg(_dg(mix(4), a1_ref[...]).astype(BF16), a2_ref[...]))
    gate_ref[...] = _dg(_sigmoid(_dg(mix(5), g1_ref[...])).astype(BF16), g2_ref[...])

    lane_lo = lax.broadcasted_iota(jnp.int32, (tm, LANES), 1) < HEAD_DIM
    kk = k * k_k
    for p in range(kk.shape[1] // LANES):
        sl = slice(p * LANES, (p + 1) * LANES)
        kkp = kk[:, sl]
        norm = jnp.maximum(jnp.sqrt(_group_sum(kkp * kkp, lane_lo)), L2_EPS)
        kkp = kkp / norm
        kk_ref[:, sl] = kkp
        b_ref[:, sl] = kkp * a[:, sl]
    k_ref[...] = k * (1.0 + (a - 1.0) * k_a)


def _rwkv_prep_call(h2d, vf2d, g, mu, vecs, wrkv_bf, loras, tm, tp):
    m, d = h2d.shape
    row_spec = pl.BlockSpec((tm, d), lambda i: (i, 0))
    in_specs = [row_spec,
                pl.BlockSpec((8, d), lambda i: (jnp.maximum(i * (tm // 8) - 1, 0), 0)),
                row_spec,
                _const_spec((1, d)), _const_spec(mu.shape), _const_spec(vecs.shape),
                _const_spec(wrkv_bf.shape)]
    in_specs += [_const_spec(w.shape) for w in loras]
    return pl.pallas_call(
        functools.partial(_rwkv_prep_kernel, tm=tm, tp=tp),
        grid=(m // tm,),
        in_specs=in_specs,
        out_specs=[row_spec] * 7,
        out_shape=[jax.ShapeDtypeStruct((m, d), F32)] * 7,
        compiler_params=pltpu.CompilerParams(
            dimension_semantics=("parallel",), vmem_limit_bytes=VMEM_LIMIT),
        name="rwkv_prep",
    )(h2d, h2d, vf2d, g, mu, vecs, wrkv_bf, *loras)


def _stack_heads(x, lane_lo):
    zero = jnp.zeros_like(x)
    return jnp.concatenate([jnp.where(lane_lo, x, zero), jnp.where(lane_lo, zero, x)], axis=0)


def _rwkv_chunk_setup(rs, ks, vs, lws, kks, bs, cums, passes):
    c = CHUNK
    n = 2 * c
    units = range(len(rs))
    lane_lo = lax.broadcasted_iota(jnp.int32, (c, LANES), 1) < HEAD_DIM
    st = functools.partial(_stack_heads, lane_lo=lane_lo)
    row = lax.broadcasted_iota(jnp.int32, (n, n), 0)
    col = lax.broadcasted_iota(jnp.int32, (n, n), 1)
    strict = (col % c) < (row % c)
    incl = (col % c) <= (row % c)
    in_lo = (row < c) & (col < c)
    in_hi = (row >= c) & (col >= c)
    eye = (row == col).astype(F32)

    def mm(a, b, dn=_NN):
        return _mm(a, b, dn, passes)

    def block_diag(g, tri, swap):
        g_sw = pltpu.roll(g, c, axis=1)
        lo, hi = (g_sw, g) if swap else (g, g_sw)
        return jnp.where(tri & in_lo, lo, jnp.where(tri & in_hi, hi, 0.0))

    e_inv = [jnp.exp(-cums[u]) for u in units]
    a_t = [-kks[u] * jnp.exp(cums[u] - lws[u]) for u in units]
    r_t = [rs[u] * jnp.exp(cums[u]) for u in units]
    s_v = [st(vs[u]) for u in units]
    gram = [mm(jnp.concatenate([st(a_t[u]), st(r_t[u])], axis=0),
               jnp.concatenate([bs[u] * e_inv[u], ks[u] * e_inv[u]], axis=0), _NT)
            for u in units]
    a_ab = [block_diag(gram[u][:n], strict, False) for u in units]
    ak_v = [mm(block_diag(gram[u][:n], strict, True), s_v[u]) for u in units]

    inv = [eye + a_ab[u] for u in units]
    power = a_ab
    for _ in range((c - 1).bit_length() - 1):
        power = [mm(power[u], power[u]) for u in units]
        inv = [inv[u] + mm(inv[u], power[u]) for u in units]

    totals = [cums[u][c - 1:c, :] for u in units]
    e_rem = [jnp.exp(totals[u] - cums[u]) for u in units]
    return dict(
        ar=[jnp.concatenate([a_t[u], r_t[u]], axis=0) for u in units],
        s_v=s_v, ak_v=ak_v, inv=inv,
        m_r=[jnp.concatenate([block_diag(gram[u][n:], incl, False),
                              block_diag(gram[u][n:], incl, True)], axis=1) for u in units],
        bk_rem=[jnp.concatenate([st(bs[u] * e_rem[u]), st(ks[u] * e_rem[u])], axis=0)
                for u in units],
        decay=[jnp.exp(totals[u]) for u in units])


def _rwkv_chunk_apply(setup, units, states, passes):
    c = CHUNK
    lane_lo = lax.broadcasted_iota(jnp.int32, (c, LANES), 1) < HEAD_DIM
    st = functools.partial(_stack_heads, lane_lo=lane_lo)
    idx = range(len(units))

    def mm(a, b, dn=_NN):
        return _mm(a, b, dn, passes)

    x1 = [mm(setup["ar"][units[i]], states[i], _NT) for i in idx]
    rhs = [st(x1[i][:c]) + setup["ak_v"][units[i]] for i in idx]
    s_uv = [jnp.concatenate([mm(setup["inv"][units[i]], rhs[i]), setup["s_v"][units[i]]], axis=0)
            for i in idx]
    s_y = [st(x1[i][c:]) + mm(setup["m_r"][units[i]], s_uv[i]) for i in idx]
    ys = [s_y[i][:c] + s_y[i][c:] for i in idx]
    new_states = [states[i] * setup["decay"][units[i]]
                  + mm(s_uv[i], setup["bk_rem"][units[i]], _TN) for i in idx]
    return ys, new_states


def _rwkv_scan_kernel(r_ref, k_ref, v_ref, lw_ref, kk_ref, b_ref, gate_ref,
                      rk_ref, lnw_ref, lnb_ref, out_ref, state_ref, *, passes):
    c = CHUNK
    n_seqs, rows = lw_ref.shape[0], lw_ref.shape[1]
    n_chunks = rows // c
    n_groups = lw_ref.shape[2] // LANES
    n_chains = n_seqs * n_groups

    @pl.when(pl.program_id(2) == 0)
    def _():
        state_ref[...] = jnp.zeros_like(state_ref)

    rowc = lax.broadcasted_iota(jnp.int32, (rows, rows), 0)
    colc = lax.broadcasted_iota(jnp.int32, (rows, rows), 1)
    chunk_tril = ((colc <= rowc) & (colc // c == rowc // c)).astype(BF16)
    lw = [lw_ref[s] for s in range(n_seqs)]
    cum = [_mm_exact_lhs(chunk_tril, lw[s], n=3) for s in range(n_seqs)]

    windows = [(s, slice(ch * c, (ch + 1) * c), slice(p * LANES, (p + 1) * LANES))
               for ch in range(n_chunks) for s in range(n_seqs) for p in range(n_groups)]
    units = range(len(windows))
    rs = [r_ref[w] for w in windows]
    ks = [k_ref[w] for w in windows]
    vs = [v_ref[w] for w in windows]
    setup = _rwkv_chunk_setup(rs, ks, vs, [lw[w[0]][w[1:]] for w in windows],
                              [kk_ref[w] for w in windows], [b_ref[w] for w in windows],
                              [cum[w[0]][w[1:]] for w in windows], passes)
    states = [state_ref[q] for q in range(n_chains)]
    ys = []
    for ch in range(n_chunks):
        y_ch, states = _rwkv_chunk_apply(
            setup, [ch * n_chains + q for q in range(n_chains)], states, passes)
        ys += y_ch
    for q in range(n_chains):
        state_ref[q] = states[q]

    lane_lo = lax.broadcasted_iota(jnp.int32, (c, LANES), 1) < HEAD_DIM
    inv_n = 1.0 / HEAD_DIM
    means = [_group_sum(ys[u], lane_lo) * inv_n for u in units]
    devs = [ys[u] - means[u] for u in units]
    variances = [_group_sum(devs[u] * devs[u], lane_lo) * inv_n for u in units]
    bonus = [_group_sum(rs[u] * ks[u] * rk_ref[:, windows[u][2]], lane_lo) * vs[u] for u in units]
    for u in units:
        w = windows[u]
        yn = devs[u] * lax.rsqrt(variances[u] + GN_EPS) * lnw_ref[:, w[2]] + lnb_ref[:, w[2]]
        out_ref[w] = ((yn + bonus[u]) * gate_ref[w]).astype(BF16)


def _rwkv_scan_call(seqs, rk, lnw, lnb, lane_group, chunks_per_step, seqs_per_step, passes):
    b, tp, d = seqs[0].shape
    c = CHUNK * chunks_per_step
    seq_spec = pl.BlockSpec((seqs_per_step, c, lane_group), lambda bi, l, ci: (bi, ci, l))
    vec_spec = pl.BlockSpec((1, lane_group), lambda bi, l, ci: (0, l))
    return pl.pallas_call(
        functools.partial(_rwkv_scan_kernel, passes=passes),
        grid=(b // seqs_per_step, d // lane_group, tp // c),
        in_specs=[seq_spec] * 7 + [vec_spec] * 3,
        out_specs=seq_spec,
        out_shape=jax.ShapeDtypeStruct((b, tp, d), BF16),
        scratch_shapes=[pltpu.VMEM((seqs_per_step * lane_group // LANES, LANES, LANES), F32)],
        compiler_params=pltpu.CompilerParams(
            dimension_semantics=("parallel", "parallel", "arbitrary"),
            vmem_limit_bytes=VMEM_LIMIT),
        name="rwkv_scan",
    )(*seqs, rk, lnw, lnb)


def kernel(x, meta_tokens, attn_norm, w_qkv, w_o_attn, rwkv_norm, rwkv_mu, w_rkv, w_o_rwkv,
           w0, w1, w2, a0, a1, a2, v0, v1, v2, g1, g2, k_k, k_a, r_k, ln_x_w, ln_x_b,
           ffn_norm, w_ffn_in, w_ffn_out, final_norm):
    b, seq, d = x.shape
    n_meta = meta_tokens.shape[0]
    pad = (-n_meta) % ATT_BLOCK
    tp = pad + n_meta + seq
    assert pad + n_meta == ATT_BLOCK and seq % ATT_BLOCK == 0 and d % LANES == 0
    m = b * tp
    blocks_per_seq = tp // ATT_BLOCK
    tm = 512 if m % 512 == 0 else ATT_BLOCK
    tm_prep = 384 if m % 384 == 0 else ATT_BLOCK

    def row(vec):
        return vec.reshape(1, d).astype(F32)

    lead = jnp.concatenate([jnp.zeros((pad, d), x.dtype), meta_tokens.astype(x.dtype)], axis=0)

    qkv, v_first, (wo0, win0, wout0) = _qkv_call(
        x, lead, row(attn_norm[0]), w_qkv[0].astype(BF16), tm, blocks_per_seq,
        [(w_o_attn, 0), (w_ffn_in, 0), (w_ffn_out, 0)])
    q_blocks = 3 if blocks_per_seq % 3 == 0 else 1
    o = _attn_call(qkv.reshape(b, tp, 3 * d), pad, 4 * LANES, q_blocks).reshape(m, d)
    h, (wrkv, wo1, win1, wout1) = _mix_ffn_call(
        x, lead, o, wo0, row(ffn_norm[0]), win0, wout0, tm, blocks_per_seq,
        [(w_rkv[0].reshape(1, 3 * d, d), 0), (w_o_rwkv, 0), (w_ffn_in, 1), (w_ffn_out, 1)])

    vecs = jnp.concatenate([row(w0[0]), row(a0[0]), row(v0[0]), row(k_k[0]), row(k_a[0]),
                            jnp.zeros((3, d), F32)], axis=0)
    loras = [w.astype(BF16) for w in (w1[0], w2[0], a1[0], a2[0], v1[0], v2[0], g1[0], g2[0])]
    seqs = _rwkv_prep_call(h, v_first, row(rwkv_norm[0]), rwkv_mu[0], vecs,
                           wrkv.reshape(3, d, d), loras, tm_prep, tp)
    chunks_per_step = 3 if (tp // CHUNK) % 3 == 0 else 2
    y = _rwkv_scan_call([s.reshape(b, tp, d) for s in seqs], row(r_k[0]), row(ln_x_w[0]),
                        row(ln_x_b[0]), 8 * LANES, chunks_per_step, 2 if b % 2 == 0 else 1, 1)
    tm_out = 512 if seq % 512 == 0 else ATT_BLOCK
    return _mix_ffn_final_call(h.reshape(b, tp, d), y, wo1, row(ffn_norm[1]), win1, wout1,
                               row(final_norm), pad + n_meta, tm_out)
```

```python
import functools

import jax
import jax.numpy as jnp
from jax import lax
from jax.experimental import pallas as pl
from jax.experimental.pallas import tpu as pltpu

F32 = jnp.float32
BF16 = jnp.bfloat16

HEAD_DIM = 64
N_META = 16
ATT_BLOCK = 128
KEY_BLOCKS_PER_TRIP = 2
LANES = 128
BF16_SUBLANES = 16
CHUNK = 64
RMS_EPS = 1e-6
GN_EPS = 64e-5
L2_EPS = 1e-12
EXP2_UNDERFLOW = -127.0
LOG2_E = 1.4426950408889634
VMEM_LIMIT = 56 * 1024 * 1024

_NN = (((1,), (0,)), ((), ()))
_NT = (((1,), (1,)), ((), ()))
_TN = (((0,), (0,)), ((), ()))


def _dg(a, b, dn=_NN):
    return lax.dot_general(a, b, dn, preferred_element_type=F32)


def _split(x, n):
    parts = []
    for _ in range(n):
        p = x.astype(BF16)
        parts.append(p)
        x = x - p.astype(F32)
    return parts


def _mm(a, b, dn=_NN, passes=3):
    if passes == 1:
        return _dg(a.astype(BF16), b.astype(BF16), dn)
    ah, al = _split(a, 2)
    bh, bl = _split(b, 2)
    return _dg(ah, bh, dn) + (_dg(ah, bl, dn) + _dg(al, bh, dn))


def _mm_exact_lhs(a_bf, b, dn=_NN, n=3):
    parts = _split(b, n)
    out = _dg(a_bf, parts[-1], dn)
    for p in parts[-2::-1]:
        out = out + _dg(a_bf, p, dn)
    return out


def _mm_exact_rhs(a, b_bf, dn=_NN, n=3):
    parts = _split(a, n)
    out = _dg(parts[-1], b_bf, dn)
    for p in parts[-2::-1]:
        out = out + _dg(p, b_bf, dn)
    return out


def _rms(x, g):
    return x * lax.rsqrt(jnp.mean(x * x, axis=-1, keepdims=True) + RMS_EPS) * g


def _softplus_neg_abs(z):
    return jnp.log(1.0 + jnp.exp(-jnp.abs(z)))


def _sigmoid(x):
    return 1.0 / (1.0 + jnp.exp(-x))


def _const_spec(shape):
    nd = len(shape)
    return pl.BlockSpec(shape, lambda *_: (0,) * nd, pipeline_mode=pl.Buffered(1))


def _cast_tile_rows(n_rows, n_steps):
    for rows in range(BF16_SUBLANES, n_rows + 1, BF16_SUBLANES):
        if n_rows % rows == 0 and n_rows // rows <= n_steps:
            return rows
    raise ValueError(f"cannot tile {n_rows} rows over {n_steps} steps")


def _cast_specs(weights, n_steps):
    in_specs, out_specs, out_shapes, n_tiles = [], [], [], []
    for w, layer in weights:
        _, n_rows, n_cols = w.shape
        rows = _cast_tile_rows(n_rows, n_steps)
        last = n_rows // rows - 1
        n_tiles.append(last + 1)
        in_specs.append(pl.BlockSpec(
            (None, rows, n_cols), lambda i, layer=layer, last=last: (layer, jnp.minimum(i, last), 0)))
        out_specs.append(pl.BlockSpec(
            (rows, n_cols), lambda i, last=last: (jnp.minimum(i, last), 0)))
        out_shapes.append(jax.ShapeDtypeStruct((n_rows, n_cols), BF16))
    return in_specs, out_specs, out_shapes, tuple(n_tiles)


def _cast_weight_tiles(w_refs, wbf_refs, n_tiles):
    step = pl.program_id(0)
    for w_ref, wbf_ref, n in zip(w_refs, wbf_refs, n_tiles):
        @pl.when(step < n)
        def _(w_ref=w_ref, wbf_ref=wbf_ref):
            wbf_ref[...] = w_ref[...].astype(BF16)


def _padded_piece_specs(pieces, blocks_per_seq, d):
    def spec(q):
        def index_map(i):
            blk_id = i * pieces + q
            return blk_id // blocks_per_seq, jnp.maximum(blk_id % blocks_per_seq - 1, 0), 0
        return pl.BlockSpec((None, ATT_BLOCK, d), index_map)
    return [spec(q) for q in range(pieces)]


def _padded_tile(piece_refs, lead_ref, blocks_per_seq):
    i = pl.program_id(0)
    pieces = len(piece_refs)
    parts = [jnp.where((i * pieces + q) % blocks_per_seq == 0, lead_ref[...], r[...])
             for q, r in enumerate(piece_refs)]
    return jnp.concatenate(parts, axis=0)


def _qkv_kernel(*refs, d, pieces, blocks_per_seq, cast_tiles):
    n_cast = len(cast_tiles)
    lead_ref, g_ref, w32_ref = refs[pieces:pieces + 3]
    cast_in = refs[pieces + 3:pieces + 3 + n_cast]
    qkv_ref, v_ref = refs[pieces + 3 + n_cast:pieces + 5 + n_cast]
    w_ref = refs[-1]

    @pl.when(pl.program_id(0) == 0)
    def _():
        w_ref[...] = w32_ref[...].astype(BF16)

    _cast_weight_tiles(cast_in, refs[pieces + 5 + n_cast:-1], cast_tiles)
    x = _padded_tile(refs[:pieces], lead_ref, blocks_per_seq)
    xb = _rms(x, g_ref[...]).astype(BF16)
    for j in range(3):
        y = _dg(xb, w_ref[:, j * d:(j + 1) * d])
        if j == 0:
            y = y * (HEAD_DIM ** -0.5)
        qkv_ref[:, j * d:(j + 1) * d] = y.astype(BF16)
        if j == 2:
            v_ref[...] = y


def _qkv_call(x, lead, g, w_qkv, layer, tm, blocks_per_seq, cast_weights):
    b, _, d = x.shape
    m = b * blocks_per_seq * ATT_BLOCK
    pieces = tm // ATT_BLOCK
    c_in, c_out, c_shapes, c_tiles = _cast_specs(cast_weights, m // tm)
    outs = pl.pallas_call(
        functools.partial(_qkv_kernel, d=d, pieces=pieces, blocks_per_seq=blocks_per_seq,
                          cast_tiles=c_tiles),
        grid=(m // tm,),
        in_specs=_padded_piece_specs(pieces, blocks_per_seq, d) + [
            _const_spec((ATT_BLOCK, d)), _const_spec((1, d)),
            pl.BlockSpec((None, d, 3 * d), lambda i: (layer, 0, 0),
                         pipeline_mode=pl.Buffered(1))] + c_in,
        out_specs=[pl.BlockSpec((tm, 3 * d), lambda i: (i, 0)),
                   pl.BlockSpec((tm, d), lambda i: (i, 0))] + c_out,
        out_shape=[jax.ShapeDtypeStruct((m, 3 * d), BF16),
                   jax.ShapeDtypeStruct((m, d), F32)] + c_shapes,
        scratch_shapes=[pltpu.VMEM((d, 3 * d), BF16)],
        compiler_params=pltpu.CompilerParams(
            dimension_semantics=("arbitrary",), vmem_limit_bytes=VMEM_LIMIT),
        name="qkv",
    )(*([x] * pieces), lead, g, w_qkv, *[w for w, _ in cast_weights])
    return outs[0], outs[1], outs[2:]


def _attn_kernel(q_ref, k_ref, v_ref, o_ref, *, pad):
    blk = ATT_BLOCK
    n_groups = q_ref.shape[1] // LANES
    q_blocks = q_ref.shape[0] // blk
    row = lax.broadcasted_iota(jnp.int32, (blk, blk), 0)
    col = lax.broadcasted_iota(jnp.int32, (blk, blk), 1)
    lane_lo = lax.broadcasted_iota(jnp.int32, (blk, LANES), 1) < HEAD_DIM
    upper = (row > col).astype(BF16)
    init = (jnp.int32(0), jnp.float32(0.0),
            tuple(jnp.zeros((blk, LANES), F32) for _ in range(n_groups)),
            tuple(jnp.zeros((blk, 1), F32) for _ in range(2 * n_groups)))

    loops = [_attn_query_block(q_ref, k_ref, v_ref, sub, pad, row, col, lane_lo, upper)
             for sub in range(q_blocks)]
    firsts = [body(init) for _, body in loops]
    for sub, ((cond, body), first) in enumerate(zip(loops, firsts)):
        accs = lax.while_loop(cond, body, first)[2]
        for p in range(n_groups):
            o_ref[sub * blk:(sub + 1) * blk, p * LANES:(p + 1) * LANES] = accs[p].astype(BF16)


def _attn_query_block(q_ref, k_ref, v_ref, sub, pad, row, col, lane_lo, upper):
    blk = ATT_BLOCK
    n_groups = q_ref.shape[1] // LANES
    i = pl.program_id(2) * (q_ref.shape[0] // blk) + sub
    q_heads = []
    for p in range(n_groups):
        q = q_ref[sub * blk:(sub + 1) * blk, p * LANES:(p + 1) * LANES]
        zero_q = jnp.zeros_like(q)
        q_heads += [jnp.where(lane_lo, q, zero_q), jnp.where(lane_lo, zero_q, q)]

    def cond(carry):
        jj, c_max = carry[0], carry[1]
        return (jj <= i) & (c_max > EXP2_UNDERFLOW)

    def body(carry):
        jj, _, accs, cs = carry
        n_heads = 2 * n_groups
        masks, kts, vts = [], [], []
        for w in range(KEY_BLOCKS_PER_TRIP):
            j = i - jj - w
            start = pl.multiple_of(jnp.maximum(j, 0) * blk, blk)
            s_idx = j * blk + col
            masks.append((s_idx < i * blk + row) & (s_idx >= pad))
            kts.append([k_ref[pl.ds(start, blk), p * LANES:(p + 1) * LANES] for p in range(n_groups)])
            vts.append([v_ref[pl.ds(start, blk), p * LANES:(p + 1) * LANES] for p in range(n_groups)])
        units = [(w, h) for w in range(KEY_BLOCKS_PER_TRIP) for h in range(n_heads)]
        zs = [_dg(q_heads[h], kts[w][h // 2], _NT) * LOG2_E for w, h in units]
        log_beta = [jnp.minimum(z, 0.0) - jnp.log2(1.0 + jnp.exp2(-jnp.abs(z))) for z in zs]
        log_rest = [jnp.where(masks[w], log_beta[u] - zs[u], 0.0)
                    for u, (w, h) in enumerate(units)]
        suffix = [_mm_exact_rhs(lr, upper, n=2) for lr in log_rest]
        row_tot = [jnp.sum(lr, axis=1, keepdims=True) for lr in log_rest]
        carried = list(cs)
        outs = []
        for u, (w, h) in enumerate(units):
            att = jnp.where(masks[w], jnp.exp2(log_beta[u] + suffix[u] + carried[h]), 0.0)
            outs.append(_dg(att.astype(BF16), vts[w][h // 2]))
            carried[h] = carried[h] + row_tot[u]
        new_accs = list(accs)
        for w in range(KEY_BLOCKS_PER_TRIP):
            for p in range(n_groups):
                new_accs[p] = new_accs[p] + jnp.where(lane_lo, outs[w * n_heads + 2 * p],
                                                      outs[w * n_heads + 2 * p + 1])
        c_all = carried[0]
        for c in carried[1:]:
            c_all = jnp.maximum(c_all, c)
        return jj + KEY_BLOCKS_PER_TRIP, jnp.max(c_all), tuple(new_accs), tuple(carried)

    return cond, body


def _attn_call(qkv3d, pad, lane_set, q_blocks):
    b, tp, d3 = qkv3d.shape
    d = d3 // 3
    ng = d // lane_set
    blk = ATT_BLOCK * q_blocks
    return pl.pallas_call(
        functools.partial(_attn_kernel, pad=pad),
        grid=(b, ng, tp // blk),
        in_specs=[pl.BlockSpec((None, blk, lane_set), lambda bi, p, i: (bi, i, p)),
                  pl.BlockSpec((None, tp, lane_set), lambda bi, p, i: (bi, 0, ng + p)),
                  pl.BlockSpec((None, tp, lane_set), lambda bi, p, i: (bi, 0, 2 * ng + p))],
        out_specs=pl.BlockSpec((None, blk, lane_set), lambda bi, p, i: (bi, i, p)),
        out_shape=jax.ShapeDtypeStruct((b, tp, d), BF16),
        compiler_params=pltpu.CompilerParams(
            dimension_semantics=("parallel", "parallel", "arbitrary"),
            vmem_limit_bytes=VMEM_LIMIT),
        name="attention",
    )(qkv3d, qkv3d, qkv3d)


def _mix_ffn_body(h, o, wo_ref, g_ref, win_ref, wout_ref, f):
    h1 = h + _dg(o, wo_ref[...])
    xb = _rms(h1, g_ref[...]).astype(BF16)
    gu = _dg(xb, win_ref[...])
    gate = gu[:, :f]
    act = (gate * _sigmoid(gate) * gu[:, f:]).astype(BF16)
    return h1 + _dg(act, wout_ref[...])


def _mix_ffn_kernel(*refs, f, pieces, blocks_per_seq, cast_tiles):
    n_cast = len(cast_tiles)
    lead_ref, o_ref, wo_ref, g_ref, win_ref, wout_ref = refs[pieces:pieces + 6]
    cast_in = refs[pieces + 6:pieces + 6 + n_cast]
    out_ref = refs[pieces + 6 + n_cast]
    _cast_weight_tiles(cast_in, refs[pieces + 7 + n_cast:], cast_tiles)
    h = _padded_tile(refs[:pieces], lead_ref, blocks_per_seq)
    out_ref[...] = _mix_ffn_body(h, o_ref[...], wo_ref, g_ref, win_ref, wout_ref, f)


def _mix_ffn_final_kernel(*refs, f, pieces):
    h_refs, o_refs = refs[:pieces], refs[pieces:2 * pieces]
    wo_ref, g_ref, win_ref, wout_ref, fg_ref, out_ref = refs[2 * pieces:]
    h = jnp.concatenate([r[...] for r in h_refs], axis=0)
    o = jnp.concatenate([r[...] for r in o_refs], axis=0)
    out = _mix_ffn_body(h, o, wo_ref, g_ref, win_ref, wout_ref, f)
    out_ref[...] = _rms(out, fg_ref[...])


def _mix_ffn_call(x, lead, o2d, wo_bf, g, win_bf, wout_bf, tm, blocks_per_seq, cast_weights):
    m, d = o2d.shape
    f = wout_bf.shape[0]
    pieces = tm // ATT_BLOCK
    c_in, c_out, c_shapes, c_tiles = _cast_specs(cast_weights, m // tm)
    outs = pl.pallas_call(
        functools.partial(_mix_ffn_kernel, f=f, pieces=pieces, blocks_per_seq=blocks_per_seq,
                          cast_tiles=c_tiles),
        grid=(m // tm,),
        in_specs=_padded_piece_specs(pieces, blocks_per_seq, d) + [
            _const_spec((ATT_BLOCK, d)),
            pl.BlockSpec((tm, d), lambda i: (i, 0)),
            _const_spec((d, d)),
            _const_spec((1, d)),
            _const_spec((d, 2 * f)),
            _const_spec((f, d))] + c_in,
        out_specs=[pl.BlockSpec((tm, d), lambda i: (i, 0))] + c_out,
        out_shape=[jax.ShapeDtypeStruct((m, d), F32)] + c_shapes,
        compiler_params=pltpu.CompilerParams(
            dimension_semantics=("arbitrary",), vmem_limit_bytes=VMEM_LIMIT),
        name="mix_ffn",
    )(*([x] * pieces), lead, o2d, wo_bf, g, win_bf, wout_bf, *[w for w, _ in cast_weights])
    return outs[0], outs[1:]


def _mix_ffn_final_call(h3d, o3d, wo_bf, g, win_bf, wout_bf, final_g, skip, tm):
    b, tp, d = h3d.shape
    f = wout_bf.shape[0]
    blk = ATT_BLOCK
    pieces = tm // blk
    first = skip // blk

    def piece_spec(q):
        return pl.BlockSpec((None, blk, d), lambda bi, j: (bi, first + j * pieces + q, 0))

    in_specs = [piece_spec(q) for q in range(pieces)] * 2
    in_specs += [_const_spec((d, d)), _const_spec((1, d)), _const_spec((d, 2 * f)),
                 _const_spec((f, d)), _const_spec((1, d))]
    return pl.pallas_call(
        functools.partial(_mix_ffn_final_kernel, f=f, pieces=pieces),
        grid=(b, (tp - skip) // tm),
        in_specs=in_specs,
        out_specs=pl.BlockSpec((None, tm, d), lambda bi, j: (bi, j, 0)),
        out_shape=jax.ShapeDtypeStruct((b, tp - skip, d), F32),
        compiler_params=pltpu.CompilerParams(
            dimension_semantics=("parallel", "parallel"), vmem_limit_bytes=VMEM_LIMIT),
        name="mix_ffn_final",
    )(*([h3d] * pieces), *([o3d] * pieces), wo_bf, g, win_bf, wout_bf, final_g)


def _group_sum(x, lane_lo):
    lo = jnp.sum(jnp.where(lane_lo, x, 0.0), axis=1, keepdims=True)
    hi = jnp.sum(jnp.where(lane_lo, 0.0, x), axis=1, keepdims=True)
    return jnp.where(lane_lo, lo, hi)


def _rwkv_prep_kernel(h_ref, hprev_ref, vf_ref, g_ref, mu_ref, vec_ref, wrkv_ref,
                      w1_ref, w2_ref, a1_ref, a2_ref, v1_ref, v2_ref, g1_ref, g2_ref,
                      r_ref, k_ref, v_ref, lw_ref, kk_ref, b_ref, gate_ref, *, tm, tp):
    i = pl.program_id(0)
    g = g_ref[...]
    hn = _rms(h_ref[...], g)
    prev_last = _rms(hprev_ref[...], g)[7:8, :]
    local = lax.broadcasted_iota(jnp.int32, (tm, 1), 0)
    prev = jnp.where(local == 0, prev_last, pltpu.roll(hn, 1, axis=0))
    xx = jnp.where((i * tm + local) % tp == 0, 0.0, prev - hn)

    def mix(n):
        return (hn + xx * mu_ref[n:n + 1, :]).astype(BF16)

    w0, a0, v0 = vec_ref[0:1, :], vec_ref[1:2, :], vec_ref[2:3, :]
    k_k, k_a = vec_ref[3:4, :], vec_ref[4:5, :]

    r_ref[...] = _dg(mix(0), wrkv_ref[0])
    k = _dg(mix(1), wrkv_ref[1])
    xv = mix(2)
    v = _dg(xv, wrkv_ref[2])

    wl = w0 + _dg(jnp.tanh(_dg(mix(3), w1_ref[...])).astype(BF16), w2_ref[...])
    w_log = -(jnp.maximum(-wl, 0.0) + _softplus_neg_abs(wl)) - 0.5
    lw_ref[...] = -jnp.exp(w_log)

    v_gate = _sigmoid(v0 + _dg(_dg(xv, v1_ref[...]).astype(BF16), v2_ref[...]))
    v_ref[...] = v + (vf_ref[...] - v) * v_gate
    a = _sigmoid(a0 + _dg(_dg(mix(4), a1_ref[...]).astype(BF16), a2_ref[...]))
    gate_ref[...] = _dg(_sigmoid(_dg(mix(5), g1_ref[...])).astype(BF16), g2_ref[...])

    lane_lo = lax.broadcasted_iota(jnp.int32, (tm, LANES), 1) < HEAD_DIM
    kk = k * k_k
    for p in range(kk.shape[1] // LANES):
        sl = slice(p * LANES, (p + 1) * LANES)
        kkp = kk[:, sl]
        norm = jnp.maximum(jnp.sqrt(_group_sum(kkp * kkp, lane_lo)), L2_EPS)
        kkp = kkp / norm
        kk_ref[:, sl] = kkp
        b_ref[:, sl] = kkp * a[:, sl]
    k_ref[...] = k * (1.0 + (a - 1.0) * k_a)


def _rwkv_prep_call(h2d, vf2d, g, mu, vecs, wrkv_bf, loras, tm, tp):
    m, d = h2d.shape
    row_spec = pl.BlockSpec((tm, d), lambda i: (i, 0))
    in_specs = [row_spec,
                pl.BlockSpec((8, d), lambda i: (jnp.maximum(i * (tm // 8) - 1, 0), 0)),
                row_spec,
                _const_spec((1, d)), _const_spec(mu.shape), _const_spec(vecs.shape),
                _const_spec(wrkv_bf.shape)]
    in_specs += [_const_spec(w.shape) for w in loras]
    return pl.pallas_call(
        functools.partial(_rwkv_prep_kernel, tm=tm, tp=tp),
        grid=(m // tm,),
        in_specs=in_specs,
        out_specs=[row_spec] * 7,
        out_shape=[jax.ShapeDtypeStruct((m, d), F32)] * 7,
        compiler_params=pltpu.CompilerParams(
            dimension_semantics=("parallel",), vmem_limit_bytes=VMEM_LIMIT),
        name="rwkv_prep",
    )(h2d, h2d, vf2d, g, mu, vecs, wrkv_bf, *loras)


def _stack_heads(x, lane_lo):
    zero = jnp.zeros_like(x)
    return jnp.concatenate([jnp.where(lane_lo, x, zero), jnp.where(lane_lo, zero, x)], axis=0)


def _rwkv_chunk_setup(rs, ks, vs, lws, kks, bs, cums, passes):
    c = CHUNK
    n = 2 * c
    units = range(len(rs))
    lane_lo = lax.broadcasted_iota(jnp.int32, (c, LANES), 1) < HEAD_DIM
    st = functools.partial(_stack_heads, lane_lo=lane_lo)
    row = lax.broadcasted_iota(jnp.int32, (n, n), 0)
    col = lax.broadcasted_iota(jnp.int32, (n, n), 1)
    strict = (col % c) < (row % c)
    incl = (col % c) <= (row % c)
    in_lo = (row < c) & (col < c)
    in_hi = (row >= c) & (col >= c)
    eye = (row == col).astype(F32)

    def mm(a, b, dn=_NN):
        return _mm(a, b, dn, passes)

    def block_diag(g, tri, swap):
        g_sw = pltpu.roll(g, c, axis=1)
        lo, hi = (g_sw, g) if swap else (g, g_sw)
        return jnp.where(tri & in_lo, lo, jnp.where(tri & in_hi, hi, 0.0))

    e_inv = [jnp.exp(-cums[u]) for u in units]
    a_t = [-kks[u] * jnp.exp(cums[u] - lws[u]) for u in units]
    r_t = [rs[u] * jnp.exp(cums[u]) for u in units]
    s_v = [st(vs[u]) for u in units]
    gram = [mm(jnp.concatenate([st(a_t[u]), st(r_t[u])], axis=0),
               jnp.concatenate([bs[u] * e_inv[u], ks[u] * e_inv[u]], axis=0), _NT)
            for u in units]
    a_ab = [block_diag(gram[u][:n], strict, False) for u in units]
    ak_v = [mm(block_diag(gram[u][:n], strict, True), s_v[u]) for u in units]

    inv = [eye + a_ab[u] for u in units]
    power = a_ab
    for _ in range((c - 1).bit_length() - 1):
        power = [mm(power[u], power[u]) for u in units]
        inv = [inv[u] + mm(inv[u], power[u]) for u in units]

    totals = [cums[u][c - 1:c, :] for u in units]
    e_rem = [jnp.exp(totals[u] - cums[u]) for u in units]
    return dict(
        ar=[jnp.concatenate([a_t[u], r_t[u]], axis=0) for u in units],
        s_v=s_v, ak_v=ak_v, inv=inv,
        m_r=[jnp.concatenate([block_diag(gram[u][n:], incl, False),
                              block_diag(gram[u][n:], incl, True)], axis=1) for u in units],
        bk_rem=[jnp.concatenate([st(bs[u] * e_rem[u]), st(ks[u] * e_rem[u])], axis=0)
                for u in units],
        decay=[jnp.exp(totals[u]) for u in units])


def _rwkv_chunk_apply(setup, units, states, passes):
    c = CHUNK
    lane_lo = lax.broadcasted_iota(jnp.int32, (c, LANES), 1) < HEAD_DIM
    st = functools.partial(_stack_heads, lane_lo=lane_lo)
    idx = range(len(units))

    def mm(a, b, dn=_NN):
        return _mm(a, b, dn, passes)

    x1 = [mm(setup["ar"][units[i]], states[i], _NT) for i in idx]
    rhs = [st(x1[i][:c]) + setup["ak_v"][units[i]] for i in idx]
    s_uv = [jnp.concatenate([mm(setup["inv"][units[i]], rhs[i]), setup["s_v"][units[i]]], axis=0)
            for i in idx]
    s_y = [st(x1[i][c:]) + mm(setup["m_r"][units[i]], s_uv[i]) for i in idx]
    ys = [s_y[i][:c] + s_y[i][c:] for i in idx]
    new_states = [states[i] * setup["decay"][units[i]]
                  + mm(s_uv[i], setup["bk_rem"][units[i]], _TN) for i in idx]
    return ys, new_states


def _rwkv_scan_kernel(r_ref, k_ref, v_ref, lw_ref, kk_ref, b_ref, gate_ref,
                      rk_ref, lnw_ref, lnb_ref, out_ref, state_ref, *, passes):
    c = CHUNK
    n_seqs, rows = lw_ref.shape[0], lw_ref.shape[1]
    n_chunks = rows // c
    n_groups = lw_ref.shape[2] // LANES
    n_chains = n_seqs * n_groups

    @pl.when(pl.program_id(2) == 0)
    def _():
        state_ref[...] = jnp.zeros_like(state_ref)

    rowc = lax.broadcasted_iota(jnp.int32, (rows, rows), 0)
    colc = lax.broadcasted_iota(jnp.int32, (rows, rows), 1)
    chunk_tril = ((colc <= rowc) & (colc // c == rowc // c)).astype(BF16)
    lw = [lw_ref[s] for s in range(n_seqs)]
    cum = [_mm_exact_lhs(chunk_tril, lw[s], n=3) for s in range(n_seqs)]

    windows = [(s, slice(ch * c, (ch + 1) * c), slice(p * LANES, (p + 1) * LANES))
               for ch in range(n_chunks) for s in range(n_seqs) for p in range(n_groups)]
    units = range(len(windows))
    rs = [r_ref[w] for w in windows]
    ks = [k_ref[w] for w in windows]
    vs = [v_ref[w] for w in windows]
    setup = _rwkv_chunk_setup(rs, ks, vs, [lw[w[0]][w[1:]] for w in windows],
                              [kk_ref[w] for w in windows], [b_ref[w] for w in windows],
                              [cum[w[0]][w[1:]] for w in windows], passes)
    states = [state_ref[q] for q in range(n_chains)]
    ys = []
    for ch in range(n_chunks):
        y_ch, states = _rwkv_chunk_apply(
            setup, [ch * n_chains + q for q in range(n_chains)], states, passes)
        ys += y_ch
    for q in range(n_chains):
        state_ref[q] = states[q]

    lane_lo = lax.broadcasted_iota(jnp.int32, (c, LANES), 1) < HEAD_DIM
    inv_n = 1.0 / HEAD_DIM
    means = [_group_sum(ys[u], lane_lo) * inv_n for u in units]
    devs = [ys[u] - means[u] for u in units]
    variances = [_group_sum(devs[u] * devs[u], lane_lo) * inv_n for u in units]
    bonus = [_group_sum(rs[u] * ks[u] * rk_ref[:, windows[u][2]], lane_lo) * vs[u] for u in units]
    for u in units:
        w = windows[u]
        yn = devs[u] * lax.rsqrt(variances[u] + GN_EPS) * lnw_ref[:, w[2]] + lnb_ref[:, w[2]]
        out_ref[w] = ((yn + bonus[u]) * gate_ref[w]).astype(BF16)


def _rwkv_scan_call(seqs, rk, lnw, lnb, lane_group, chunks_per_step, seqs_per_step, passes):
    b, tp, d = seqs[0].shape
    c = CHUNK * chunks_per_step
    seq_spec = pl.BlockSpec((seqs_per_step, c, lane_group), lambda bi, l, ci: (bi, ci, l))
    vec_spec = pl.BlockSpec((1, lane_group), lambda bi, l, ci: (0, l))
    return pl.pallas_call(
        functools.partial(_rwkv_scan_kernel, passes=passes),
        grid=(b // seqs_per_step, d // lane_group, tp // c),
        in_specs=[seq_spec] * 7 + [vec_spec] * 3,
        out_specs=seq_spec,
        out_shape=jax.ShapeDtypeStruct((b, tp, d), BF16),
        scratch_shapes=[pltpu.VMEM((seqs_per_step * lane_group // LANES, LANES, LANES), F32)],
        compiler_params=pltpu.CompilerParams(
            dimension_semantics=("parallel", "parallel", "arbitrary"),
            vmem_limit_bytes=VMEM_LIMIT),
        name="rwkv_scan",
    )(*seqs, rk, lnw, lnb)


def kernel(x, meta_tokens, attn_norm, w_qkv, w_o_attn, rwkv_norm, rwkv_mu, w_rkv, w_o_rwkv,
           w0, w1, w2, a0, a1, a2, v0, v1, v2, g1, g2, k_k, k_a, r_k, ln_x_w, ln_x_b,
           ffn_norm, w_ffn_in, w_ffn_out, final_norm):
    b, seq, d = x.shape
    n_meta = meta_tokens.shape[0]
    pad = (-n_meta) % ATT_BLOCK
    tp = pad + n_meta + seq
    assert pad + n_meta == ATT_BLOCK and seq % ATT_BLOCK == 0 and d % LANES == 0
    m = b * tp
    blocks_per_seq = tp // ATT_BLOCK
    tm = 512 if m % 512 == 0 else ATT_BLOCK
    tm_prep = 384 if m % 384 == 0 else ATT_BLOCK

    def row(vec):
        return vec.reshape(1, d).astype(F32)

    lead = jnp.concatenate([jnp.zeros((pad, d), x.dtype), meta_tokens.astype(x.dtype)], axis=0)

    qkv, v_first, (wo0, win0, wout0) = _qkv_call(
        x, lead, row(attn_norm[0]), w_qkv, 0, tm, blocks_per_seq,
        [(w_o_attn, 0), (w_ffn_in, 0), (w_ffn_out, 0)])
    q_blocks = 3 if blocks_per_seq % 3 == 0 else 1
    o = _attn_call(qkv.reshape(b, tp, 3 * d), pad, 4 * LANES, q_blocks).reshape(m, d)
    h, (wrkv, wo1, win1, wout1) = _mix_ffn_call(
        x, lead, o, wo0, row(ffn_norm[0]), win0, wout0, tm, blocks_per_seq,
        [(w_rkv[0].reshape(1, 3 * d, d), 0), (w_o_rwkv, 0), (w_ffn_in, 1), (w_ffn_out, 1)])

    vecs = jnp.concatenate([row(w0[0]), row(a0[0]), row(v0[0]), row(k_k[0]), row(k_a[0]),
                            jnp.zeros((3, d), F32)], axis=0)
    loras = [w.astype(BF16) for w in (w1[0], w2[0], a1[0], a2[0], v1[0], v2[0], g1[0], g2[0])]
    seqs = _rwkv_prep_call(h, v_first, row(rwkv_norm[0]), rwkv_mu[0], vecs,
                           wrkv.reshape(3, d, d), loras, tm_prep, tp)
    chunks_per_step = 3 if (tp // CHUNK) % 3 == 0 else 2
    y = _rwkv_scan_call([s.reshape(b, tp, d) for s in seqs], row(r_k[0]), row(ln_x_w[0]),
                        row(ln_x_b[0]), 8 * LANES, chunks_per_step, 2 if b % 2 == 0 else 1, 1)
    tm_out = 512 if seq % 512 == 0 else ATT_BLOCK
    return _mix_ffn_final_call(h.reshape(b, tp, d), y, wo1, row(ffn_norm[1]), win1, wout1,
                               row(final_norm), pad + n_meta, tm_out)
```

```python
import functools

import jax
import jax.numpy as jnp
from jax import lax
from jax.experimental import pallas as pl
from jax.experimental.pallas import tpu as pltpu

F32 = jnp.float32
BF16 = jnp.bfloat16

HEAD_DIM = 64
N_META = 16
ATT_BLOCK = 128
KEY_BLOCKS_PER_TRIP = 2
LANES = 128
BF16_SUBLANES = 16
CHUNK = 64
RMS_EPS = 1e-6
GN_EPS = 64e-5
L2_EPS = 1e-12
EXP2_UNDERFLOW = -127.0
LOG2_E = 1.4426950408889634
VMEM_LIMIT = 56 * 1024 * 1024

_NN = (((1,), (0,)), ((), ()))
_NT = (((1,), (1,)), ((), ()))
_TN = (((0,), (0,)), ((), ()))


def _dg(a, b, dn=_NN):
    return lax.dot_general(a, b, dn, preferred_element_type=F32)


def _split(x, n):
    parts = []
    for _ in range(n):
        p = x.astype(BF16)
        parts.append(p)
        x = x - p.astype(F32)
    return parts


def _mm(a, b, dn=_NN, passes=3):
    if passes == 1:
        return _dg(a.astype(BF16), b.astype(BF16), dn)
    ah, al = _split(a, 2)
    bh, bl = _split(b, 2)
    return _dg(ah, bh, dn) + (_dg(ah, bl, dn) + _dg(al, bh, dn))


def _mm_exact_lhs(a_bf, b, dn=_NN, n=3):
    parts = _split(b, n)
    out = _dg(a_bf, parts[-1], dn)
    for p in parts[-2::-1]:
        out = out + _dg(a_bf, p, dn)
    return out


def _mm_exact_rhs(a, b_bf, dn=_NN, n=3):
    parts = _split(a, n)
    out = _dg(parts[-1], b_bf, dn)
    for p in parts[-2::-1]:
        out = out + _dg(p, b_bf, dn)
    return out


def _rms(x, g):
    return x * lax.rsqrt(jnp.mean(x * x, axis=-1, keepdims=True) + RMS_EPS) * g


def _softplus_neg_abs(z):
    return jnp.log(1.0 + jnp.exp(-jnp.abs(z)))


def _sigmoid(x):
    return 1.0 / (1.0 + jnp.exp(-x))


def _const_spec(shape):
    nd = len(shape)
    return pl.BlockSpec(shape, lambda *_: (0,) * nd, pipeline_mode=pl.Buffered(1))


def _cast_tile_rows(n_rows, n_steps):
    for rows in range(BF16_SUBLANES, n_rows + 1, BF16_SUBLANES):
        if n_rows % rows == 0 and n_rows // rows <= n_steps:
            return rows
    raise ValueError(f"cannot tile {n_rows} rows over {n_steps} steps")


def _cast_specs(weights, n_steps):
    in_specs, out_specs, out_shapes, n_tiles = [], [], [], []
    for w, layer in weights:
        _, n_rows, n_cols = w.shape
        rows = _cast_tile_rows(n_rows, n_steps)
        last = n_rows // rows - 1
        n_tiles.append(last + 1)
        in_specs.append(pl.BlockSpec(
            (None, rows, n_cols), lambda i, layer=layer, last=last: (layer, jnp.minimum(i, last), 0)))
        out_specs.append(pl.BlockSpec(
            (rows, n_cols), lambda i, last=last: (jnp.minimum(i, last), 0)))
        out_shapes.append(jax.ShapeDtypeStruct((n_rows, n_cols), BF16))
    return in_specs, out_specs, out_shapes, tuple(n_tiles)


def _cast_weight_tiles(w_refs, wbf_refs, n_tiles):
    step = pl.program_id(0)
    for w_ref, wbf_ref, n in zip(w_refs, wbf_refs, n_tiles):
        @pl.when(step < n)
        def _(w_ref=w_ref, wbf_ref=wbf_ref):
            wbf_ref[...] = w_ref[...].astype(BF16)


def _padded_piece_specs(pieces, blocks_per_seq, d):
    def spec(q):
        def index_map(i):
            blk_id = i * pieces + q
            return blk_id // blocks_per_seq, jnp.maximum(blk_id % blocks_per_seq - 1, 0), 0
        return pl.BlockSpec((None, ATT_BLOCK, d), index_map)
    return [spec(q) for q in range(pieces)]


def _padded_tile(piece_refs, lead_ref, blocks_per_seq):
    i = pl.program_id(0)
    pieces = len(piece_refs)
    parts = [jnp.where((i * pieces + q) % blocks_per_seq == 0, lead_ref[...], r[...])
             for q, r in enumerate(piece_refs)]
    return jnp.concatenate(parts, axis=0)


def _qkv_kernel(*refs, d, pieces, blocks_per_seq, cast_tiles):
    n_cast = len(cast_tiles)
    lead_ref, g_ref, w32_ref = refs[pieces:pieces + 3]
    cast_in = refs[pieces + 3:pieces + 3 + n_cast]
    qkv_ref, v_ref = refs[pieces + 3 + n_cast:pieces + 5 + n_cast]
    w_ref = refs[-1]

    @pl.when(pl.program_id(0) == 0)
    def _():
        w_ref[...] = w32_ref[...].astype(BF16)

    _cast_weight_tiles(cast_in, refs[pieces + 5 + n_cast:-1], cast_tiles)
    x = _padded_tile(refs[:pieces], lead_ref, blocks_per_seq)
    xb = _rms(x, g_ref[...]).astype(BF16)
    for j in range(3):
        y = _dg(xb, w_ref[:, j * d:(j + 1) * d])
        if j == 0:
            y = y * (HEAD_DIM ** -0.5)
        qkv_ref[:, j * d:(j + 1) * d] = y.astype(BF16)
        if j == 2:
            v_ref[...] = y


def _qkv_call(x, lead, g, w_qkv, layer, tm, blocks_per_seq, cast_weights):
    b, _, d = x.shape
    m = b * blocks_per_seq * ATT_BLOCK
    pieces = tm // ATT_BLOCK
    c_in, c_out, c_shapes, c_tiles = _cast_specs(cast_weights, m // tm)
    outs = pl.pallas_call(
        functools.partial(_qkv_kernel, d=d, pieces=pieces, blocks_per_seq=blocks_per_seq,
                          cast_tiles=c_tiles),
        grid=(m // tm,),
        in_specs=_padded_piece_specs(pieces, blocks_per_seq, d) + [
            _const_spec((ATT_BLOCK, d)), _const_spec((1, d)),
            pl.BlockSpec((None, d, 3 * d), lambda i: (layer, 0, 0),
                         pipeline_mode=pl.Buffered(1))] + c_in,
        out_specs=[pl.BlockSpec((tm, 3 * d), lambda i: (i, 0)),
                   pl.BlockSpec((tm, d), lambda i: (i, 0))] + c_out,
        out_shape=[jax.ShapeDtypeStruct((m, 3 * d), BF16),
                   jax.ShapeDtypeStruct((m, d), F32)] + c_shapes,
        scratch_shapes=[pltpu.VMEM((d, 3 * d), BF16)],
        compiler_params=pltpu.CompilerParams(
            dimension_semantics=("arbitrary",), vmem_limit_bytes=VMEM_LIMIT),
        name="qkv",
    )(*([x] * pieces), lead, g, w_qkv, *[w for w, _ in cast_weights])
    return outs[0], outs[1], outs[2:]


def _attn_kernel(q_ref, k_ref, v_ref, o_ref, *, pad):
    blk = ATT_BLOCK
    n_groups = q_ref.shape[1] // LANES
    q_blocks = q_ref.shape[0] // blk
    row = lax.broadcasted_iota(jnp.int32, (blk, blk), 0)
    col = lax.broadcasted_iota(jnp.int32, (blk, blk), 1)
    lane_lo = lax.broadcasted_iota(jnp.int32, (blk, LANES), 1) < HEAD_DIM
    upper = (row > col).astype(BF16)
    init = (jnp.int32(0), jnp.float32(0.0),
            tuple(jnp.zeros((blk, LANES), F32) for _ in range(n_groups)),
            tuple(jnp.zeros((blk, 1), F32) for _ in range(2 * n_groups)))

    loops = [_attn_query_block(q_ref, k_ref, v_ref, sub, pad, row, col, lane_lo, upper)
             for sub in range(q_blocks)]
    firsts = [body(init) for _, body in loops]
    for sub, ((cond, body), first) in enumerate(zip(loops, firsts)):
        accs = lax.while_loop(cond, body, first)[2]
        for p in range(n_groups):
            o_ref[sub * blk:(sub + 1) * blk, p * LANES:(p + 1) * LANES] = accs[p].astype(BF16)


def _attn_query_block(q_ref, k_ref, v_ref, sub, pad, row, col, lane_lo, upper):
    blk = ATT_BLOCK
    n_groups = q_ref.shape[1] // LANES
    i = pl.program_id(2) * (q_ref.shape[0] // blk) + sub
    q_heads = []
    for p in range(n_groups):
        q = q_ref[sub * blk:(sub + 1) * blk, p * LANES:(p + 1) * LANES]
        zero_q = jnp.zeros_like(q)
        q_heads += [jnp.where(lane_lo, q, zero_q), jnp.where(lane_lo, zero_q, q)]

    def cond(carry):
        jj, c_max = carry[0], carry[1]
        return (jj <= i) & (c_max > EXP2_UNDERFLOW)

    def body(carry):
        jj, _, accs, cs = carry
        n_heads = 2 * n_groups
        masks, kts, vts = [], [], []
        for w in range(KEY_BLOCKS_PER_TRIP):
            j = i - jj - w
            start = pl.multiple_of(jnp.maximum(j, 0) * blk, blk)
            s_idx = j * blk + col
            masks.append((s_idx < i * blk + row) & (s_idx >= pad))
            kts.append([k_ref[pl.ds(start, blk), p * LANES:(p + 1) * LANES] for p in range(n_groups)])
            vts.append([v_ref[pl.ds(start, blk), p * LANES:(p + 1) * LANES] for p in range(n_groups)])
        units = [(w, h) for w in range(KEY_BLOCKS_PER_TRIP) for h in range(n_heads)]
        zs = [_dg(q_heads[h], kts[w][h // 2], _NT) * LOG2_E for w, h in units]
        log_beta = [jnp.minimum(z, 0.0) - jnp.log2(1.0 + jnp.exp2(-jnp.abs(z))) for z in zs]
        log_rest = [jnp.where(masks[w], log_beta[u] - zs[u], 0.0)
                    for u, (w, h) in enumerate(units)]
        suffix = [_mm_exact_rhs(lr, upper, n=2) for lr in log_rest]
        row_tot = [jnp.sum(lr, axis=1, keepdims=True) for lr in log_rest]
        carried = list(cs)
        outs = []
        for u, (w, h) in enumerate(units):
            att = jnp.where(masks[w], jnp.exp2(log_beta[u] + suffix[u] + carried[h]), 0.0)
            outs.append(_dg(att.astype(BF16), vts[w][h // 2]))
            carried[h] = carried[h] + row_tot[u]
        new_accs = list(accs)
        for w in range(KEY_BLOCKS_PER_TRIP):
            for p in range(n_groups):
                new_accs[p] = new_accs[p] + jnp.where(lane_lo, outs[w * n_heads + 2 * p],
                                                      outs[w * n_heads + 2 * p + 1])
        c_all = carried[0]
        for c in carried[1:]:
            c_all = jnp.maximum(c_all, c)
        return jj + KEY_BLOCKS_PER_TRIP, jnp.max(c_all), tuple(new_accs), tuple(carried)

    return cond, body


def _attn_call(qkv3d, pad, lane_set, q_blocks):
    b, tp, d3 = qkv3d.shape
    d = d3 // 3
    ng = d // lane_set
    blk = ATT_BLOCK * q_blocks
    return pl.pallas_call(
        functools.partial(_attn_kernel, pad=pad),
        grid=(b, ng, tp // blk),
        in_specs=[pl.BlockSpec((None, blk, lane_set), lambda bi, p, i: (bi, i, p)),
                  pl.BlockSpec((None, tp, lane_set), lambda bi, p, i: (bi, 0, ng + p)),
                  pl.BlockSpec((None, tp, lane_set), lambda bi, p, i: (bi, 0, 2 * ng + p))],
        out_specs=pl.BlockSpec((None, blk, lane_set), lambda bi, p, i: (bi, i, p)),
        out_shape=jax.ShapeDtypeStruct((b, tp, d), BF16),
        compiler_params=pltpu.CompilerParams(
            dimension_semantics=("parallel", "parallel", "arbitrary"),
            vmem_limit_bytes=VMEM_LIMIT),
        name="attention",
    )(qkv3d, qkv3d, qkv3d)


def _mix_ffn_body(h, o, wo_ref, g_ref, win_ref, wout_ref, f):
    h1 = h + _dg(o, wo_ref[...])
    xb = _rms(h1, g_ref[...]).astype(BF16)
    gu = _dg(xb, win_ref[...])
    gate = gu[:, :f]
    act = (gate * _sigmoid(gate) * gu[:, f:]).astype(BF16)
    return h1 + _dg(act, wout_ref[...])


def _mix_ffn_kernel(*refs, f, pieces, blocks_per_seq, cast_tiles):
    n_cast = len(cast_tiles)
    lead_ref, o_ref, wo_ref, g_ref, win_ref, wout_ref = refs[pieces:pieces + 6]
    cast_in = refs[pieces + 6:pieces + 6 + n_cast]
    out_ref = refs[pieces + 6 + n_cast]
    _cast_weight_tiles(cast_in, refs[pieces + 7 + n_cast:], cast_tiles)
    h = _padded_tile(refs[:pieces], lead_ref, blocks_per_seq)
    out_ref[...] = _mix_ffn_body(h, o_ref[...], wo_ref, g_ref, win_ref, wout_ref, f)


def _mix_ffn_final_kernel(*refs, f, pieces):
    h_refs, o_refs = refs[:pieces], refs[pieces:2 * pieces]
    wo_ref, g_ref, win_ref, wout_ref, fg_ref, out_ref = refs[2 * pieces:]
    h = jnp.concatenate([r[...] for r in h_refs], axis=0)
    o = jnp.concatenate([r[...] for r in o_refs], axis=0)
    out = _mix_ffn_body(h, o, wo_ref, g_ref, win_ref, wout_ref, f)
    out_ref[...] = _rms(out, fg_ref[...])


def _mix_ffn_call(x, lead, o2d, wo_bf, g, win_bf, wout_bf, tm, blocks_per_seq, cast_weights):
    m, d = o2d.shape
    f = wout_bf.shape[0]
    pieces = tm // ATT_BLOCK
    c_in, c_out, c_shapes, c_tiles = _cast_specs(cast_weights, m // tm)
    outs = pl.pallas_call(
        functools.partial(_mix_ffn_kernel, f=f, pieces=pieces, blocks_per_seq=blocks_per_seq,
                          cast_tiles=c_tiles),
        grid=(m // tm,),
        in_specs=_padded_piece_specs(pieces, blocks_per_seq, d) + [
            _const_spec((ATT_BLOCK, d)),
            pl.BlockSpec((tm, d), lambda i: (i, 0)),
            _const_spec((d, d)),
            _const_spec((1, d)),
            _const_spec((d, 2 * f)),
            _const_spec((f, d))] + c_in,
        out_specs=[pl.BlockSpec((tm, d), lambda i: (i, 0))] + c_out,
        out_shape=[jax.ShapeDtypeStruct((m, d), F32)] + c_shapes,
        compiler_params=pltpu.CompilerParams(
            dimension_semantics=("arbitrary",), vmem_limit_bytes=VMEM_LIMIT),
        name="mix_ffn",
    )(*([x] * pieces), lead, o2d, wo_bf, g, win_bf, wout_bf, *[w for w, _ in cast_weights])
    return outs[0], outs[1:]


def _mix_ffn_final_call(h3d, o3d, wo_bf, g, win_bf, wout_bf, final_g, skip, tm):
    b, tp, d = h3d.shape
    f = wout_bf.shape[0]
    blk = ATT_BLOCK
    pieces = tm // blk
    first = skip // blk

    def piece_spec(q):
        return pl.BlockSpec((None, blk, d), lambda bi, j: (bi, first + j * pieces + q, 0))

    in_specs = [piece_spec(q) for q in range(pieces)] * 2
    in_specs += [_const_spec((d, d)), _const_spec((1, d)), _const_spec((d, 2 * f)),
                 _const_spec((f, d)), _const_spec((1, d))]
    return pl.pallas_call(
        functools.partial(_mix_ffn_final_kernel, f=f, pieces=pieces),
        grid=(b, (tp - skip) // tm),
        in_specs=in_specs,
        out_specs=pl.BlockSpec((None, tm, d), lambda bi, j: (bi, j, 0)),
        out_shape=jax.ShapeDtypeStruct((b, tp - skip, d), F32),
        compiler_params=pltpu.CompilerParams(
            dimension_semantics=("parallel", "parallel"), vmem_limit_bytes=VMEM_LIMIT),
        name="mix_ffn_final",
    )(*([h3d] * pieces), *([o3d] * pieces), wo_bf, g, win_bf, wout_bf, final_g)


def _group_sum(x, lane_lo):
    lo = jnp.sum(jnp.where(lane_lo, x, 0.0), axis=1, keepdims=True)
    hi = jnp.sum(jnp.where(lane_lo, 0.0, x), axis=1, keepdims=True)
    return jnp.where(lane_lo, lo, hi)


def _rwkv_prep_kernel(h_ref, hprev_ref, vf_ref, g_ref, mu_ref, vec_ref, wrkv_ref,
                      w1_ref, w2_ref, a1_ref, a2_ref, v1_ref, v2_ref, g1_ref, g2_ref,
                      r_ref, k_ref, v_ref, lw_ref, kk_ref, b_ref, gate_ref, *, tm, tp):
    i = pl.program_id(0)
    g = g_ref[...]
    hn = _rms(h_ref[...], g)
    prev_last = _rms(hprev_ref[...], g)[7:8, :]
    local = lax.broadcasted_iota(jnp.int32, (tm, 1), 0)
    prev = jnp.where(local == 0, prev_last, pltpu.roll(hn, 1, axis=0))
    xx = jnp.where((i * tm + local) % tp == 0, 0.0, prev - hn)

    def mix(n):
        return (hn + xx * mu_ref[n:n + 1, :]).astype(BF16)

    w0, a0, v0 = vec_ref[0:1, :], vec_ref[1:2, :], vec_ref[2:3, :]
    k_k, k_a = vec_ref[3:4, :], vec_ref[4:5, :]

    r_ref[...] = _dg(mix(0), wrkv_ref[0])
    k = _dg(mix(1), wrkv_ref[1])
    xv = mix(2)
    v = _dg(xv, wrkv_ref[2])

    wl = w0 + _dg(jnp.tanh(_dg(mix(3), w1_ref[...])).astype(BF16), w2_ref[...])
    w_log = -(jnp.maximum(-wl, 0.0) + _softplus_neg_abs(wl)) - 0.5
    lw_ref[...] = -jnp.exp(w_log)

    v_gate = _sigmoid(v0 + _dg(_dg(xv, v1_ref[...]).astype(BF16), v2_ref[...]))
    v_ref[...] = v + (vf_ref[...] - v) * v_gate
    a = _sigmoid(a0 + _dg(_dg(mix(4), a1_ref[...]).astype(BF16), a2_ref[...]))
    gate_ref[...] = _dg(_sigmoid(_dg(mix(5), g1_ref[...])).astype(BF16), g2_ref[...])

    lane_lo = lax.broadcasted_iota(jnp.int32, (tm, LANES), 1) < HEAD_DIM
    kk = k * k_k
    for p in range(kk.shape[1] // LANES):
        sl = slice(p * LANES, (p + 1) * LANES)
        kkp = kk[:, sl]
        norm = jnp.maximum(jnp.sqrt(_group_sum(kkp * kkp, lane_lo)), L2_EPS)
        kkp = kkp / norm
        kk_ref[:, sl] = kkp
        b_ref[:, sl] = kkp * a[:, sl]
    k_ref[...] = k * (1.0 + (a - 1.0) * k_a)


def _rwkv_prep_call(h2d, vf2d, g, mu, vecs, wrkv_bf, loras, tm, tp):
    m, d = h2d.shape
    row_spec = pl.BlockSpec((tm, d), lambda i: (i, 0))
    in_specs = [row_spec,
                pl.BlockSpec((8, d), lambda i: (jnp.maximum(i * (tm // 8) - 1, 0), 0)),
                row_spec,
                _const_spec((1, d)), _const_spec(mu.shape), _const_spec(vecs.shape),
                _const_spec(wrkv_bf.shape)]
    in_specs += [_const_spec(w.shape) for w in loras]
    return pl.pallas_call(
        functools.partial(_rwkv_prep_kernel, tm=tm, tp=tp),
        grid=(m // tm,),
        in_specs=in_specs,
        out_specs=[row_spec] * 7,
        out_shape=[jax.ShapeDtypeStruct((m, d), F32)] * 7,
        compiler_params=pltpu.CompilerParams(
            dimension_semantics=("parallel",), vmem_limit_bytes=VMEM_LIMIT),
        name="rwkv_prep",
    )(h2d, h2d, vf2d, g, mu, vecs, wrkv_bf, *loras)


def _stack_heads(x, lane_lo):
    zero = jnp.zeros_like(x)
    return jnp.concatenate([jnp.where(lane_lo, x, zero), jnp.where(lane_lo, zero, x)], axis=0)


def _rwkv_chunk_setup(rs, ks, vs, lws, kks, bs, cums, passes):
    c = CHUNK
    n = 2 * c
    units = range(len(rs))
    lane_lo = lax.broadcasted_iota(jnp.int32, (c, LANES), 1) < HEAD_DIM
    st = functools.partial(_stack_heads, lane_lo=lane_lo)
    row = lax.broadcasted_iota(jnp.int32, (n, n), 0)
    col = lax.broadcasted_iota(jnp.int32, (n, n), 1)
    strict = (col % c) < (row % c)
    incl = (col % c) <= (row % c)
    in_lo = (row < c) & (col < c)
    in_hi = (row >= c) & (col >= c)
    eye = (row == col).astype(F32)

    def mm(a, b, dn=_NN):
        return _mm(a, b, dn, passes)

    def block_diag(g, tri, swap):
        g_sw = pltpu.roll(g, c, axis=1)
        lo, hi = (g_sw, g) if swap else (g, g_sw)
        return jnp.where(tri & in_lo, lo, jnp.where(tri & in_hi, hi, 0.0))

    e_inv = [jnp.exp(-cums[u]) for u in units]
    a_t = [-kks[u] * jnp.exp(cums[u] - lws[u]) for u in units]
    r_t = [rs[u] * jnp.exp(cums[u]) for u in units]
    s_v = [st(vs[u]) for u in units]
    gram = [mm(jnp.concatenate([st(a_t[u]), st(r_t[u])], axis=0),
               jnp.concatenate([bs[u] * e_inv[u], ks[u] * e_inv[u]], axis=0), _NT)
            for u in units]
    a_ab = [block_diag(gram[u][:n], strict, False) for u in units]
    ak_v = [mm(block_diag(gram[u][:n], strict, True), s_v[u]) for u in units]

    inv = [eye + a_ab[u] for u in units]
    power = a_ab
    for _ in range((c - 1).bit_length() - 1):
        power = [mm(power[u], power[u]) for u in units]
        inv = [inv[u] + mm(inv[u], power[u]) for u in units]

    totals = [cums[u][c - 1:c, :] for u in units]
    e_rem = [jnp.exp(totals[u] - cums[u]) for u in units]
    return dict(
        ar=[jnp.concatenate([a_t[u], r_t[u]], axis=0) for u in units],
        s_v=s_v, ak_v=ak_v, inv=inv,
        m_r=[jnp.concatenate([block_diag(gram[u][n:], incl, False),
                              block_diag(gram[u][n:], incl, True)], axis=1) for u in units],
        bk_rem=[jnp.concatenate([st(bs[u] * e_rem[u]), st(ks[u] * e_rem[u])], axis=0)
                for u in units],
        decay=[jnp.exp(totals[u]) for u in units])


def _rwkv_chunk_apply(setup, units, states, passes):
    c = CHUNK
    lane_lo = lax.broadcasted_iota(jnp.int32, (c, LANES), 1) < HEAD_DIM
    st = functools.partial(_stack_heads, lane_lo=lane_lo)
    idx = range(len(units))

    def mm(a, b, dn=_NN):
        return _mm(a, b, dn, passes)

    x1 = [mm(setup["ar"][units[i]], states[i], _NT) for i in idx]
    rhs = [st(x1[i][:c]) + setup["ak_v"][units[i]] for i in idx]
    s_uv = [jnp.concatenate([mm(setup["inv"][units[i]], rhs[i]), setup["s_v"][units[i]]], axis=0)
            for i in idx]
    s_y = [st(x1[i][c:]) + mm(setup["m_r"][units[i]], s_uv[i]) for i in idx]
    ys = [s_y[i][:c] + s_y[i][c:] for i in idx]
    new_states = [states[i] * setup["decay"][units[i]]
                  + mm(s_uv[i], setup["bk_rem"][units[i]], _TN) for i in idx]
    return ys, new_states


def _rwkv_scan_kernel(r_ref, k_ref, v_ref, lw_ref, kk_ref, b_ref, gate_ref,
                      rk_ref, lnw_ref, lnb_ref, out_ref, state_ref, *, passes):
    c = CHUNK
    n_seqs, rows = lw_ref.shape[0], lw_ref.shape[1]
    n_chunks = rows // c
    n_groups = lw_ref.shape[2] // LANES
    n_chains = n_seqs * n_groups

    @pl.when(pl.program_id(2) == 0)
    def _():
        state_ref[...] = jnp.zeros_like(state_ref)

    rowc = lax.broadcasted_iota(jnp.int32, (rows, rows), 0)
    colc = lax.broadcasted_iota(jnp.int32, (rows, rows), 1)
    chunk_tril = ((colc <= rowc) & (colc // c == rowc // c)).astype(BF16)
    lw = [lw_ref[s] for s in range(n_seqs)]
    cum = [_mm_exact_lhs(chunk_tril, lw[s], n=3) for s in range(n_seqs)]

    windows = [(s, slice(ch * c, (ch + 1) * c), slice(p * LANES, (p + 1) * LANES))
               for ch in range(n_chunks) for s in range(n_seqs) for p in range(n_groups)]
    units = range(len(windows))
    rs = [r_ref[w] for w in windows]
    ks = [k_ref[w] for w in windows]
    vs = [v_ref[w] for w in windows]
    setup = _rwkv_chunk_setup(rs, ks, vs, [lw[w[0]][w[1:]] for w in windows],
                              [kk_ref[w] for w in windows], [b_ref[w] for w in windows],
                              [cum[w[0]][w[1:]] for w in windows], passes)
    states = [state_ref[q] for q in range(n_chains)]
    ys = []
    for ch in range(n_chunks):
        y_ch, states = _rwkv_chunk_apply(
            setup, [ch * n_chains + q for q in range(n_chains)], states, passes)
        ys += y_ch
    for q in range(n_chains):
        state_ref[q] = states[q]

    lane_lo = lax.broadcasted_iota(jnp.int32, (c, LANES), 1) < HEAD_DIM
    inv_n = 1.0 / HEAD_DIM
    means = [_group_sum(ys[u], lane_lo) * inv_n for u in units]
    devs = [ys[u] - means[u] for u in units]
    variances = [_group_sum(devs[u] * devs[u], lane_lo) * inv_n for u in units]
    bonus = [_group_sum(rs[u] * ks[u] * rk_ref[:, windows[u][2]], lane_lo) * vs[u] for u in units]
    for u in units:
        w = windows[u]
        yn = devs[u] * lax.rsqrt(variances[u] + GN_EPS) * lnw_ref[:, w[2]] + lnb_ref[:, w[2]]
        out_ref[w] = ((yn + bonus[u]) * gate_ref[w]).astype(BF16)


def _rwkv_scan_call(seqs, rk, lnw, lnb, lane_group, chunks_per_step, seqs_per_step, passes):
    b, tp, d = seqs[0].shape
    c = CHUNK * chunks_per_step
    seq_spec = pl.BlockSpec((seqs_per_step, c, lane_group), lambda bi, l, ci: (bi, ci, l))
    vec_spec = pl.BlockSpec((1, lane_group), lambda bi, l, ci: (0, l))
    return pl.pallas_call(
        functools.partial(_rwkv_scan_kernel, passes=passes),
        grid=(b // seqs_per_step, d // lane_group, tp // c),
        in_specs=[seq_spec] * 7 + [vec_spec] * 3,
        out_specs=seq_spec,
        out_shape=jax.ShapeDtypeStruct((b, tp, d), BF16),
        scratch_shapes=[pltpu.VMEM((seqs_per_step * lane_group // LANES, LANES, LANES), F32)],
        compiler_params=pltpu.CompilerParams(
            dimension_semantics=("parallel", "parallel", "arbitrary"),
            vmem_limit_bytes=VMEM_LIMIT),
        name="rwkv_scan",
    )(*seqs, rk, lnw, lnb)


def kernel(x, meta_tokens, attn_norm, w_qkv, w_o_attn, rwkv_norm, rwkv_mu, w_rkv, w_o_rwkv,
           w0, w1, w2, a0, a1, a2, v0, v1, v2, g1, g2, k_k, k_a, r_k, ln_x_w, ln_x_b,
           ffn_norm, w_ffn_in, w_ffn_out, final_norm):
    b, seq, d = x.shape
    n_meta = meta_tokens.shape[0]
    pad = (-n_meta) % ATT_BLOCK
    tp = pad + n_meta + seq
    assert pad + n_meta == ATT_BLOCK and seq % ATT_BLOCK == 0 and d % LANES == 0
    m = b * tp
    blocks_per_seq = tp // ATT_BLOCK
    tm = 512 if m % 512 == 0 else ATT_BLOCK
    tm_prep = 384 if m % 384 == 0 else ATT_BLOCK

    def row(vec):
        return vec.reshape(1, d).astype(F32)

    lead = jnp.concatenate([jnp.zeros((pad, d), x.dtype), meta_tokens.astype(x.dtype)], axis=0)

    qkv, v_first, (wo0, win0, wout0) = _qkv_call(
        x, lead, row(attn_norm[0]), w_qkv, 0, 768 if m % 768 == 0 else tm, blocks_per_seq,
        [(w_o_attn, 0), (w_ffn_in, 0), (w_ffn_out, 0)])
    q_blocks = 3 if blocks_per_seq % 3 == 0 else 1
    o = _attn_call(qkv.reshape(b, tp, 3 * d), pad, 4 * LANES, q_blocks).reshape(m, d)
    h, (wrkv, wo1, win1, wout1) = _mix_ffn_call(
        x, lead, o, wo0, row(ffn_norm[0]), win0, wout0, tm, blocks_per_seq,
        [(w_rkv[0].reshape(1, 3 * d, d), 0), (w_o_rwkv, 0), (w_ffn_in, 1), (w_ffn_out, 1)])

    vecs = jnp.concatenate([row(w0[0]), row(a0[0]), row(v0[0]), row(k_k[0]), row(k_a[0]),
                            jnp.zeros((3, d), F32)], axis=0)
    loras = [w.astype(BF16) for w in (w1[0], w2[0], a1[0], a2[0], v1[0], v2[0], g1[0], g2[0])]
    seqs = _rwkv_prep_call(h, v_first, row(rwkv_norm[0]), rwkv_mu[0], vecs,
                           wrkv.reshape(3, d, d), loras, tm_prep, tp)
    chunks_per_step = 3 if (tp // CHUNK) % 3 == 0 else 2
    y = _rwkv_scan_call([s.reshape(b, tp, d) for s in seqs], row(r_k[0]), row(ln_x_w[0]),
                        row(ln_x_b[0]), 8 * LANES, chunks_per_step, 2 if b % 2 == 0 else 1, 1)
    tm_out = 512 if seq % 512 == 0 else ATT_BLOCK
    return _mix_ffn_final_call(h.reshape(b, tp, d), y, wo1, row(ffn_norm[1]), win1, wout1,
                               row(final_norm), pad + n_meta, tm_out)
```
